```python
import math
import jax, jax.numpy as jnp
from jax import lax
import numpy as np

D_MODEL = 1024
BATCH = 16
SEQ = 256
DEPTH = 2
DEC_BATCH = 8
DEC_SEQ = 4096
PAST_LEN = 256

GRID_W = 64
H_A = 4
D_A = 64
H_B = 4
D_B = 128
CONV_K = 5
CHUNK = 64
H_C = 4
D_C = 128
WIN_R = 8
WIN_C = 16
N_EXPERTS = 16
D_EXPERT = 1024
EC_FACTOR = 2
Q_BLOCK = 128
ROPE_BASE = 10000.0
EPS = 1e-6
NEG = -1e30

W_A = H_A * 2 * D_A
W_B = H_B * D_B
W_C = H_C * D_C
SPLIT_SIZES = (W_A, W_A, W_A,
               W_B, W_B, W_B, W_B,
               2 * H_B, 2 * H_B,
               W_C, W_C, W_C,
               D_MODEL, D_MODEL, D_MODEL)
D_IN = 3 * W_A + 4 * W_B + 4 * H_B + 3 * W_C + 3 * D_MODEL

kernel_name = 'hybrid_diffusion_prefix_trunk_step'


def rmsnorm(x, g):
    xf = x.astype(jnp.float32)
    y = xf * lax.rsqrt(jnp.mean(xf * xf, axis=-1, keepdims=True) + EPS)
    return (y * g.astype(jnp.float32)).astype(x.dtype)


def l2norm(x):
    xf = x.astype(jnp.float32)
    return xf * lax.rsqrt(jnp.sum(xf * xf, axis=-1, keepdims=True) + EPS)


def split_cols(u):
    out, start = [], 0
    for s in SPLIT_SIZES:
        out.append(u[..., start:start + s])
        start += s
    return out


def lambda_init(layer):
    return 0.8 - 0.6 * math.exp(-0.3 * layer)


def axial_rope(x):
    T = x.shape[1]
    quarter = x.shape[-1] // 4
    inv = 1.0 / (ROPE_BASE ** (jnp.arange(quarter, dtype=jnp.float32) / quarter))
    t = jnp.arange(T)
    pos = jnp.stack([t // GRID_W, t % GRID_W], axis=-1).astype(jnp.float32)
    ang = pos[:, :, None] * inv
    bshape = (1, T) + (1,) * (x.ndim - 3) + (2, quarter)
    cos = jnp.cos(ang).reshape(bshape).astype(x.dtype)
    sin = jnp.sin(ang).reshape(bshape).astype(x.dtype)
    xr = x.reshape(x.shape[:-1] + (2, 2, quarter))
    x1, x2 = xr[..., 0, :], xr[..., 1, :]
    return jnp.stack([x1 * cos - x2 * sin, x2 * cos + x1 * sin], axis=-2).reshape(x.shape)


def diff_attention(q, k, v, lam, g_sub, lam_init):
    B, Tq = q.shape[:2]
    scale = D_A ** -0.5
    qb = q.reshape((B, Tq // Q_BLOCK, Q_BLOCK) + q.shape[2:]).swapaxes(0, 1)

    def block(qi):
        s = jnp.einsum('bqhcd,bkhcd->bhcqk', qi, k).astype(jnp.float32) * scale
        p = jax.nn.softmax(s, axis=-1)
        a = p[:, :, 0] - lam * p[:, :, 1]
        return jnp.einsum('bhqk,bkhe->bqhe', a.astype(v.dtype), v)

    o = lax.map(block, qb).swapaxes(0, 1).reshape(B, Tq, H_A, 2 * D_A)
    o = rmsnorm(o, g_sub) * (1.0 - lam_init)
    return o.reshape(B, Tq, W_A)


def dense_attention(q, k, v):
    B, T, H, D = q.shape
    scale = D ** -0.5
    qb = q.reshape(B, T // Q_BLOCK, Q_BLOCK, H, D).swapaxes(0, 1)

    def block(qi):
        s = jnp.einsum('bqhd,bkhd->bhqk', qi, k).astype(jnp.float32) * scale
        p = jax.nn.softmax(s, axis=-1).astype(v.dtype)
        return jnp.einsum('bhqk,bkhd->bqhd', p, v)

    return lax.map(block, qb).swapaxes(0, 1).reshape(B, T, H * D)


def neighborhood_attention(q, k, v, k_ctx, v_ctx, rpb):
    B, T, H, D = q.shape
    rows = T // GRID_W
    wr = min(WIN_R, rows)
    scale = D ** -0.5
    qg = q.reshape(B, rows, GRID_W, H, D)
    kg = k.reshape(B, rows, GRID_W, H, D)
    vg = v.reshape(B, rows, GRID_W, H, D)
    col = jnp.arange(GRID_W)
    cs = jnp.clip(col - WIN_C // 2, 0, GRID_W - WIN_C)
    col_mask = (col[None, :] >= cs[:, None]) & (col[None, :] < cs[:, None] + WIN_C)
    dc_idx = jnp.clip(col[None, :] - col[:, None] + WIN_C - 1, 0, 2 * WIN_C - 2)

    def row_block(r):
        rs = jnp.clip(r - wr // 2, 0, rows - wr)
        qr = lax.dynamic_index_in_dim(qg, r, axis=1, keepdims=False)
        kb = lax.dynamic_slice_in_dim(kg, rs, wr, axis=1)
        vb = lax.dynamic_slice_in_dim(vg, rs, wr, axis=1)
        s_loc = jnp.einsum('bqhd,biwhd->bhqiw', qr, kb).astype(jnp.float32) * scale
        dr_idx = rs + jnp.arange(wr) - r + (WIN_R - 1)
        bias = rpb.astype(jnp.float32)[:, dr_idx[:, None, None], dc_idx[None, :, :]]
        s_loc = jnp.where(col_mask[None, None, :, None, :], s_loc + bias.transpose(0, 2, 1, 3)[None], NEG)
        s_ctx = jnp.einsum('bqhd,bkhd->bhqk', qr, k_ctx).astype(jnp.float32) * scale
        n_loc = wr * GRID_W
        s = jnp.concatenate([s_loc.reshape(B, H, GRID_W, n_loc), s_ctx], axis=-1)
        p = jax.nn.softmax(s, axis=-1).astype(v.dtype)
        p_loc = p[..., :n_loc].reshape(B, H, GRID_W, wr, GRID_W)
        return (jnp.einsum('bhqiw,biwhd->bqhd', p_loc, vb)
                + jnp.einsum('bhqk,bkhd->bqhd', p[..., n_loc:], v_ctx))

    out = lax.map(row_block, jnp.arange(rows))
    return out.transpose(1, 0, 2, 3, 4).reshape(B, T, H * D)


def short_conv(x, w):
    pad = CONV_K // 2
    return lax.conv_general_dilated(x, w[:, None, :].astype(x.dtype), window_strides=(1,),
                                    padding=[(pad, pad)], dimension_numbers=('NWC', 'WIO', 'NWC'),
                                    feature_group_count=x.shape[-1])


def deltanet_inputs(q, k, v, a, b, conv_w, a_log, dt_bias):
    B, T, _ = q.shape
    qkv = jax.nn.silu(short_conv(jnp.concatenate([q, k, v], axis=-1), conv_w))
    q, k, v = jnp.split(qkv, 3, axis=-1)
    q = l2norm(q.reshape(B, T, H_B, D_B))
    k = l2norm(k.reshape(B, T, H_B, D_B))
    v = v.reshape(B, T, H_B, D_B).astype(jnp.float32)
    a = a.reshape(B, T, 2, H_B).astype(jnp.float32)
    g = -jnp.exp(a_log.astype(jnp.float32)) * jax.nn.softplus(a + dt_bias.astype(jnp.float32))
    beta = jax.nn.sigmoid(b.reshape(B, T, 2, H_B).astype(jnp.float32))
    return q, k, v, g, beta


def gated_delta_chunked(q, k, v, g, beta, s0):
    B, T, H, DK = q.shape
    n = T // CHUNK
    q = q * (DK ** -0.5)
    to_c = lambda t: t.reshape(B, n, CHUNK, H, t.shape[-1]).transpose(1, 0, 3, 2, 4)
    q, k, v = to_c(q), to_c(k), to_c(v)
    G = jnp.cumsum(g.reshape(B, n, CHUNK, H).transpose(1, 0, 3, 2), axis=-1)
    beta = beta.reshape(B, n, CHUNK, H).transpose(1, 0, 3, 2)
    tri = jnp.tril(jnp.ones((CHUNK, CHUNK), dtype=bool))
    strict = jnp.tril(jnp.ones((CHUNK, CHUNK), dtype=bool), k=-1)
    diff = G[..., :, None] - G[..., None, :]
    decay = jnp.where(tri, jnp.exp(jnp.where(tri, diff, 0.0)), 0.0)
    kb = k * beta[..., None]
    vb = v * beta[..., None]
    L = jnp.where(strict, jnp.einsum('...id,...jd->...ij', kb, k) * decay, 0.0)
    eye = jnp.eye(CHUNK, dtype=jnp.float32)
    Tm = lax.linalg.triangular_solve(eye + L, jnp.broadcast_to(eye, L.shape), left_side=True,
                                     lower=True, unit_diagonal=True)
    u = jnp.einsum('...ij,...jd->...id', Tm, vb)
    w = jnp.einsum('...ij,...jd->...id', Tm, kb * jnp.exp(G)[..., None])
    a_in = jnp.where(tri, jnp.einsum('...id,...jd->...ij', q, k) * decay, 0.0)

    def step(S, xs):
        qi, ki, ui, wi, Gi, Ai = xs
        v_new = ui - jnp.einsum('bhck,bhkv->bhcv', wi, S)
        o = (jnp.einsum('bhck,bhkv->bhcv', qi * jnp.exp(Gi)[..., None], S)
             + jnp.einsum('bhij,bhjv->bhiv', Ai, v_new))
        g_last = Gi[..., -1]
        S = (S * jnp.exp(g_last)[..., None, None]
             + jnp.einsum('bhck,bhcv->bhkv', ki * jnp.exp(g_last[..., None] - Gi)[..., None], v_new))
        return S, o

    s_fin, o = lax.scan(step, s0.astype(jnp.float32), (q, k, u, w, G, a_in))
    return o.transpose(1, 0, 3, 2, 4).reshape(B, T, H, v.shape[-1]), s_fin


def bidir_gated_delta(q, k, v, g, beta, s0_f, s0_b):
    o_f, s_f = gated_delta_chunked(q, k, v, g[:, :, 0], beta[:, :, 0], s0_f)
    rev = lambda t: jnp.flip(t, axis=1)
    o_b, s_b = gated_delta_chunked(rev(q), rev(k), rev(v), rev(g[:, :, 1]), rev(beta[:, :, 1]), s0_b)
    return o_f + rev(o_b), s_f, s_b


def gated_head_norm(o, z, g):
    B, T = z.shape[:2]
    on = rmsnorm(o, g)
    return (on * jax.nn.silu(z.reshape(B, T, H_B, D_B).astype(jnp.float32))).reshape(B, T, W_B).astype(z.dtype)


def expert_choice_ffn(h, w_router, w_gate, w_up, w_down):
    B, T, D = h.shape
    n = B * T
    xt = h.reshape(n, D)
    cap = EC_FACTOR * n // N_EXPERTS
    aff = jax.nn.softmax((xt @ w_router).astype(jnp.float32), axis=-1)
    gate, idx = lax.top_k(aff.T, cap)
    xe = jnp.take(xt, idx, axis=0)
    hid = jax.nn.silu(jnp.einsum('ecd,edf->ecf', xe, w_gate)) * jnp.einsum('ecd,edf->ecf', xe, w_up)
    ye = jnp.einsum('ecf,efd->ecd', hid, w_down) * gate[..., None].astype(h.dtype)
    y = jnp.zeros_like(xt).at[idx.reshape(-1)].add(ye.reshape(-1, D))
    return y.reshape(B, T, D)


def trunk_layer(x, cond, lw, lam_init, ctx=None):
    B, T, _ = x.shape
    mod = (jax.nn.silu(cond) @ lw['w_ada'] + lw['b_ada'])[:, None, :]
    sh1, sc1, gt1, sh2, sc2, gt2 = jnp.split(mod, 6, axis=-1)
    h = rmsnorm(x, lw['g_norm1']) * (1.0 + sc1) + sh1
    (qa, ka, va, qb, kb, vb, zb, ab, bb, qc, kc, vc, ga, gb, gc) = split_cols(h @ lw['w_in'])
    qa = qa.reshape(B, T, H_A, 2, D_A)
    ka = ka.reshape(B, T, H_A, 2, D_A)
    va = va.reshape(B, T, H_A, 2 * D_A)
    qc = qc.reshape(B, T, H_C, D_C)
    kc = kc.reshape(B, T, H_C, D_C)
    vc = vc.reshape(B, T, H_C, D_C)
    lq = lw['lam_qk'].astype(jnp.float32)
    lam = jnp.exp(jnp.sum(lq[0] * lq[1])) - jnp.exp(jnp.sum(lq[2] * lq[3])) + lam_init
    qd, kd, vd, gd, betad = deltanet_inputs(qb, kb, vb, ab, bb, lw['conv_w'], lw['a_log'], lw['dt_bias'])
    if ctx is None:
        oa = diff_attention(qa, ka, va, lam, lw['g_subln'], lam_init)
        s0 = jnp.zeros((B, H_B, D_B, D_B), jnp.float32)
        od, s_f, s_b = bidir_gated_delta(qd, kd, vd, gd, betad, s0, s0)
        oc = dense_attention(qc, kc, vc)
        new_ctx = (ka.reshape(B, T, H_A, 2 * D_A), va, kc, vc,
                   jnp.stack([s_f, s_b], axis=1).astype(x.dtype))
    else:
        ck_a, cv_a, ck_c, cv_c, c_state = ctx
        Lc = ck_a.shape[1]
        k_all = jnp.concatenate([axial_rope(ka), ck_a.reshape(B, Lc, H_A, 2, D_A).astype(ka.dtype)], axis=1)
        v_all = jnp.concatenate([va, cv_a.astype(va.dtype)], axis=1)
        oa = diff_attention(axial_rope(qa), k_all, v_all, lam, lw['g_subln'], lam_init)
        od, _, _ = bidir_gated_delta(qd, kd, vd, gd, betad, c_state[:, 0], c_state[:, 1])
        oc = neighborhood_attention(qc, kc, vc, ck_c.astype(kc.dtype), cv_c.astype(vc.dtype), lw['rpb'])
        new_ctx = None
    ob = gated_head_norm(od, zb, lw['g_onorm'])
    merged = (jax.nn.sigmoid(ga) * (oa @ lw['w_br_a'])
              + jax.nn.sigmoid(gb) * (ob @ lw['w_br_b'])
              + jax.nn.sigmoid(gc) * (oc @ lw['w_br_c']))
    x = x + gt1 * (merged @ lw['w_out'])
    h2 = rmsnorm(x, lw['g_norm2']) * (1.0 + sc2) + sh2
    x = x + gt2 * expert_choice_ffn(h2, lw['w_router'], lw['w_e_gate'], lw['w_e_up'], lw['w_e_down'])
    return x, new_ctx


def setup_inputs(seed: int = 0) -> dict:
    key = jax.random.key(seed)
    keys = iter(jax.random.split(key, 40))

    def nrm(shape, scale):
        return jax.random.normal(next(keys), shape, jnp.float32) * scale

    D = D_MODEL
    dt = jnp.exp(jax.random.uniform(next(keys), (DEPTH, 2, H_B), jnp.float32, math.log(1e-3), math.log(1e-1)))
    return {
        'x_prompt': nrm((BATCH, SEQ, D), 1.0),
        'x_sample': nrm((DEC_BATCH, DEC_SEQ, D), 1.0),
        'c': nrm((DEC_BATCH, D), 1.0),
        'cache_diff_k': nrm((DEC_BATCH, DEPTH, PAST_LEN, H_A, 2 * D_A), 1.0),
        'cache_diff_v': nrm((DEC_BATCH, DEPTH, PAST_LEN, H_A, 2 * D_A), 1.0),
        'cache_na_k': nrm((DEC_BATCH, DEPTH, PAST_LEN, H_C, D_C), 1.0),
        'cache_na_v': nrm((DEC_BATCH, DEPTH, PAST_LEN, H_C, D_C), 1.0),
        'state_delta': nrm((DEC_BATCH, DEPTH, 2, H_B, D_B, D_B), 0.1),
        'c_ctx': nrm((D,), 1.0),
        'w_ada': nrm((DEPTH, D, 6 * D), 0.5 * D ** -0.5),
        'b_ada': nrm((DEPTH, 6 * D), 0.02),
        'g_norm1': 1.0 + nrm((DEPTH, D), 0.02),
        'g_norm2': 1.0 + nrm((DEPTH, D), 0.02),
        'w_in': nrm((DEPTH, D, D_IN), D ** -0.5),
        'conv_w': nrm((DEPTH, CONV_K, 3 * W_B), CONV_K ** -0.5),
        'a_log': jnp.log(jax.random.uniform(next(keys), (DEPTH, 2, H_B), jnp.float32, 1.0, 16.0)),
        'dt_bias': jnp.log(jnp.expm1(dt)),
        'g_onorm': 1.0 + nrm((DEPTH, D_B), 0.02),
        'lam_qk': nrm((DEPTH, 4, D_A), 0.1),
        'g_subln': 1.0 + nrm((DEPTH, 2 * D_A), 0.02),
        'rpb': nrm((DEPTH, H_C, 2 * WIN_R - 1, 2 * WIN_C - 1), 0.1),
        'w_br_a': nrm((DEPTH, W_A, D), W_A ** -0.5),
        'w_br_b': nrm((DEPTH, W_B, D), W_B ** -0.5),
        'w_br_c': nrm((DEPTH, W_C, D), W_C ** -0.5),
        'w_out': nrm((DEPTH, D, D), D ** -0.5),
        'w_router': nrm((DEPTH, D, N_EXPERTS), D ** -0.5),
        'w_e_gate': nrm((DEPTH, N_EXPERTS, D, D_EXPERT), D ** -0.5),
        'w_e_up': nrm((DEPTH, N_EXPERTS, D, D_EXPERT), D ** -0.5),
        'w_e_down': nrm((DEPTH, N_EXPERTS, D_EXPERT, D), D_EXPERT ** -0.5),
        'g_final': 1.0 + nrm((D,), 0.02),
    }


def reference(x_prompt, x_sample, c, cache_diff_k, cache_diff_v, cache_na_k, cache_na_v, state_delta,
              c_ctx, w_ada, b_ada, g_norm1, g_norm2, w_in, conv_w, a_log, dt_bias, g_onorm, lam_qk,
              g_subln, rpb, w_br_a, w_br_b, w_br_c, w_out, w_router, w_e_gate, w_e_up, w_e_down, g_final):
    layers = []
    for l in range(DEPTH):
        layers.append({
            'w_ada': w_ada[l], 'b_ada': b_ada[l], 'g_norm1': g_norm1[l], 'g_norm2': g_norm2[l],
            'w_in': w_in[l], 'conv_w': conv_w[l], 'a_log': a_log[l], 'dt_bias': dt_bias[l],
            'g_onorm': g_onorm[l], 'lam_qk': lam_qk[l], 'g_subln': g_subln[l], 'rpb': rpb[l],
            'w_br_a': w_br_a[l], 'w_br_b': w_br_b[l], 'w_br_c': w_br_c[l], 'w_out': w_out[l],
            'w_router': w_router[l], 'w_e_gate': w_e_gate[l], 'w_e_up': w_e_up[l], 'w_e_down': w_e_down[l],
        })

    cond_ctx = jnp.broadcast_to(c_ctx, (x_prompt.shape[0], D_MODEL))
    xp = x_prompt
    dk_l, dv_l, nk_l, nv_l, st_l = [], [], [], [], []
    for l in range(DEPTH):
        xp, (dk, dv, nk, nv, st) = trunk_layer(xp, cond_ctx, layers[l], lambda_init(l))
        dk_l.append(dk)
        dv_l.append(dv)
        nk_l.append(nk)
        nv_l.append(nv)
        st_l.append(st)
    y_prompt = rmsnorm(xp, g_final)

    xs = x_sample
    for l in range(DEPTH):
        ctx = (cache_diff_k[:, l], cache_diff_v[:, l], cache_na_k[:, l], cache_na_v[:, l], state_delta[:, l])
        xs, _ = trunk_layer(xs, c, layers[l], lambda_init(l), ctx)
    y_sample = rmsnorm(xs, g_final)

    new_diff_k = jnp.stack(dk_l, axis=1)
    new_diff_v = jnp.stack(dv_l, axis=1)
    new_na_k = jnp.stack(nk_l, axis=1)
    new_na_v = jnp.stack(nv_l, axis=1)
    new_state_delta = jnp.stack(st_l, axis=1)
    return (y_prompt, y_sample, new_diff_k, new_diff_v, new_na_k, new_na_v, new_state_delta)
```

```python
import functools
import math

import jax
import jax.numpy as jnp
from jax import lax
from jax.experimental import pallas as pl
from jax.experimental.pallas import tpu as pltpu

F32 = jnp.float32
BF16 = jnp.bfloat16
I32 = jnp.int32

H_A, D_A = 4, 64
H_B, D_B = 4, 128
H_C, D_C = 4, 128
CONV_K = 5
CHUNK = 64
GRID_W = 64
WIN_R, WIN_C = 8, 16
N_EXPERTS = 16
EC_FACTOR = 2
ROPE_BASE = 10000.0
EPS = 1e-6
NEG = -1e30
W_HEADS = 512

LANES = 128
VMEM_LIMIT_BYTES = 56 * 1024 * 1024


def _cparams(*sem):
    return pltpu.CompilerParams(dimension_semantics=sem, vmem_limit_bytes=VMEM_LIMIT_BYTES)


def _dot(a, b):
    return jnp.dot(a, b, preferred_element_type=F32)


def _dot_nt(a, b):
    return lax.dot_general(a, b, (((1,), (1,)), ((), ())), preferred_element_type=F32)


def _dot_tn(a, b):
    return lax.dot_general(a, b, (((0,), (0,)), ((), ())), preferred_element_type=F32)


def _dot_f32(a, b):
    return jnp.dot(a, b, preferred_element_type=F32, precision=lax.Precision.HIGHEST)


def _sigmoid(x):
    return 1.0 / (1.0 + jnp.exp(-x))


def _silu(x):
    return x * _sigmoid(x)


def _softplus(x):
    return jnp.maximum(x, 0.0) + jnp.log(1.0 + jnp.exp(-jnp.abs(x)))


def _rms(x, g):
    return x * lax.rsqrt(jnp.mean(x * x, axis=-1, keepdims=True) + EPS) * g


def _iota(shape, dim):
    return lax.broadcasted_iota(I32, shape, dim)


def _pick_tile(n, pref):
    t = min(n, pref)
    while n % t:
        t //= 2
    return t


def _adaln_kernel(c_ref, w_ref, b_ref, o_ref):
    c = c_ref[...]
    s = _silu(c).astype(BF16)
    o_ref[0] = _dot(s, w_ref[0].astype(BF16)) + b_ref[0]


def _adaln(cond, w_ada, b_ada):
    L, D, N = w_ada.shape
    R = cond.shape[0]
    tn = _pick_tile(N, 1536)
    return pl.pallas_call(
        _adaln_kernel,
        grid=(L, N // tn),
        in_specs=[pl.BlockSpec((R, D), lambda l, j: (0, 0)),
                  pl.BlockSpec((1, D, tn), lambda l, j: (l, 0, j)),
                  pl.BlockSpec((1, 1, tn), lambda l, j: (l, 0, j))],
        out_specs=pl.BlockSpec((1, R, tn), lambda l, j: (l, 0, j)),
        out_shape=jax.ShapeDtypeStruct((L, R, N), F32),
        compiler_params=_cparams("parallel", "parallel"),
        name="adaln",
    )(cond, w_ada, b_ada.reshape(L, 1, N))


def _inproj_kernel(*refs, rope, emit_kv):
    it = iter(refs)
    x_ref, sc_ref, sh_ref, g_ref, w_ref, wab_ref, wabT_ref = [next(it) for _ in range(7)]
    if rope:
        cos_ref, sa_ref, sb_ref = [next(it) for _ in range(3)]
    qa_ref, ka_ref, va_ref, xb_ref, zb_ref, qc_ref, kc_ref, vc_ref, ab_ref, abT_ref = [next(it) for _ in range(10)]
    if emit_kv:
        ka32_ref, va32_ref, kc32_ref, vc32_ref = [next(it) for _ in range(4)]

    x = x_ref[...]
    h = _rms(x, g_ref[...]) * (1.0 + sc_ref[0]) + sh_ref[0]
    hb = h.astype(BF16)
    W = W_HEADS

    def proj(j, n=1):
        return _dot(hb, w_ref[:, j * W:(j + n) * W])

    def roped(v):
        if not rope:
            return v
        n = v.shape[-1]
        return v * cos_ref[...] + pltpu.roll(v, n - D_A // 4, 1) * sa_ref[...] + pltpu.roll(v, D_A // 4, 1) * sb_ref[...]

    qa = proj(0)
    qa_ref[...] = (roped(qa) * (D_A ** -0.5)).astype(BF16)
    ka = proj(1)
    ka_ref[...] = roped(ka).astype(BF16)
    va = proj(2)
    va_ref[...] = va.astype(BF16)
    xb_ref[...] = proj(3, 3)
    zb_ref[...] = proj(6)
    qc_ref[...] = proj(7).astype(BF16)
    kc = proj(8)
    kc_ref[...] = kc.astype(BF16)
    vc = proj(9)
    vc_ref[...] = vc.astype(BF16)
    ab_ref[...] = _dot_f32(h, wab_ref[...])
    abT_ref[...] = lax.dot_general(wabT_ref[...], h, (((1,), (1,)), ((), ())), preferred_element_type=F32,
                                   precision=lax.Precision.HIGHEST)
    if emit_kv:
        ka32_ref[...] = ka
        va32_ref[...] = va
        kc32_ref[...] = kc
        vc32_ref[...] = vc


def _inproj(x, sc, sh, g, w_main, w_ab, w_abT, T, rope_tabs, emit_kv):
    n, D = x.shape
    tm = _pick_tile(T, 512)
    nt = T // tm
    W = W_HEADS
    rope = rope_tabs is not None
    row = lambda i: (i, 0)
    per_b = lambda i: (i // nt, 0, 0)
    const = lambda i: (0, 0)
    in_specs = [pl.BlockSpec((tm, D), row),
                pl.BlockSpec((1, 1, D), per_b), pl.BlockSpec((1, 1, D), per_b),
                pl.BlockSpec((1, D), const),
                pl.BlockSpec(w_main.shape, const), pl.BlockSpec(w_ab.shape, const), pl.BlockSpec(w_abT.shape, const)]
    args = [x, sc, sh, g, w_main, w_ab, w_abT]
    if rope:
        in_specs += [pl.BlockSpec((tm, W), lambda i: (i % nt, 0))] * 3
        args += list(rope_tabs)
    bf = lambda: jax.ShapeDtypeStruct((n, W), BF16)
    f32 = lambda w: jax.ShapeDtypeStruct((n, w), F32)
    out_shape = [bf(), bf(), bf(), f32(3 * W), f32(W), bf(), bf(), bf(), f32(16), jax.ShapeDtypeStruct((16, n), F32)]
    out_specs = [pl.BlockSpec((tm, W), row)] * 3 + [pl.BlockSpec((tm, 3 * W), row), pl.BlockSpec((tm, W), row)] \
        + [pl.BlockSpec((tm, W), row)] * 3 + [pl.BlockSpec((tm, 16), row), pl.BlockSpec((16, tm), lambda i: (0, i))]
    if emit_kv:
        out_shape += [f32(W)] * 4
        out_specs += [pl.BlockSpec((tm, W), row)] * 4
    return pl.pallas_call(
        functools.partial(_inproj_kernel, rope=rope, emit_kv=emit_kv),
        grid=(n // tm,), in_specs=in_specs, out_specs=out_specs, out_shape=out_shape,
        compiler_params=_cparams("parallel"), name="inproj",
    )(*args)


def _rope_tables(T):
    quarter = D_A // 4
    inv = 1.0 / (ROPE_BASE ** (jnp.arange(quarter, dtype=F32) / quarter))
    t = jnp.arange(T)
    pos = jnp.stack([t // GRID_W, t % GRID_W], axis=-1).astype(F32)
    ang = pos[:, :, None] * inv
    cos, sin = jnp.cos(ang), jnp.sin(ang)
    cos64 = jnp.concatenate([cos[:, 0], cos[:, 0], cos[:, 1], cos[:, 1]], axis=-1)
    zero = jnp.zeros_like(sin[:, 0])
    sa64 = jnp.concatenate([-sin[:, 0], zero, -sin[:, 1], zero], axis=-1)
    sb64 = jnp.concatenate([zero, sin[:, 0], zero, sin[:, 1]], axis=-1)
    rep = W_HEADS // D_A
    return tuple(jnp.tile(a, (1, rep)) for a in (cos64, sa64, sb64))


def _diff_attn_kernel(*refs, has_cache, lam_init):
    if has_cache:
        q_ref, k_ref, v_ref, kc_ref, vc_ref, lq_ref, g_ref, o_ref = refs
    else:
        q_ref, k_ref, v_ref, lq_ref, g_ref, o_ref = refs
    q = q_ref[...]
    lane = _iota(q.shape, 1)
    zero = jnp.zeros_like(q)
    qs = (jnp.where(lane < D_A, q, zero), jnp.where(lane >= D_A, q, zero))
    k = k_ref[...]
    lq = lq_ref[...]
    lam = (jnp.exp(jnp.sum(lq[0:1] * lq[1:2], axis=-1, keepdims=True))
           - jnp.exp(jnp.sum(lq[2:3] * lq[3:4], axis=-1, keepdims=True)) + lam_init)
    ps = []
    for comp in range(2):
        s = _dot_nt(qs[comp], k)
        m = jnp.max(s, axis=-1, keepdims=True)
        if has_cache:
            sc = _dot_nt(qs[comp], kc_ref[...])
            m = jnp.maximum(m, jnp.max(sc, axis=-1, keepdims=True))
            pc = jnp.exp(sc - m)
        p = jnp.exp(s - m)
        l = jnp.sum(p, axis=-1, keepdims=True)
        if has_cache:
            l = l + jnp.sum(pc, axis=-1, keepdims=True)
        r = 1.0 / l
        if comp == 1:
            r = r * lam
        ps.append((p * r, pc * r if has_cache else None))
    a = (ps[0][0] - ps[1][0]).astype(BF16)
    o = _dot(a, v_ref[...])
    if has_cache:
        ac = (ps[0][1] - ps[1][1]).astype(BF16)
        o = o + _dot(ac, vc_ref[...])
    o_ref[...] = (_rms(o, g_ref[...]) * (1.0 - lam_init)).astype(BF16)


def _diff_attn(q, k, v, cache_k, cache_v, lam_qk, g_sub, B, T, lam_init):
    n = q.shape[0]
    tq = _pick_tile(T, 256)
    nq = T // tq
    has_cache = cache_k is not None
    hd = 2 * D_A
    in_specs = [pl.BlockSpec((tq, hd), lambda b, h, i: (b * nq + i, h)),
                pl.BlockSpec((T, hd), lambda b, h, i: (b, h)),
                pl.BlockSpec((T, hd), lambda b, h, i: (b, h))]
    args = [q, k, v]
    if has_cache:
        Lc = cache_k.shape[0] // B
        in_specs += [pl.BlockSpec((Lc, hd), lambda b, h, i: (b, h))] * 2
        args += [cache_k, cache_v]
    in_specs += [pl.BlockSpec(lam_qk.shape, lambda b, h, i: (0, 0)), pl.BlockSpec((1, hd), lambda b, h, i: (0, 0))]
    args += [lam_qk, g_sub]
    return pl.pallas_call(
        functools.partial(_diff_attn_kernel, has_cache=has_cache, lam_init=lam_init),
        grid=(B, H_A, nq), in_specs=in_specs,
        out_specs=pl.BlockSpec((tq, hd), lambda b, h, i: (b * nq + i, h)),
        out_shape=jax.ShapeDtypeStruct((n, W_HEADS), BF16),
        compiler_params=_cparams("parallel", "parallel", "parallel"), name="diff_attn",
    )(*args)


def _softmax_pv(parts):
    m = None
    for s, _ in parts:
        mi = jnp.max(s, axis=-1, keepdims=True)
        m = mi if m is None else jnp.maximum(m, mi)
    ps = [jnp.exp(s - m) for s, _ in parts]
    l = None
    for p in ps:
        li = jnp.sum(p, axis=-1, keepdims=True)
        l = li if l is None else l + li
    r = 1.0 / l
    o = None
    for p, (_, v) in zip(ps, parts):
        oi = _dot((p * r).astype(BF16), v)
        o = oi if o is None else o + oi
    return o


def _na_kernel(q_ref, k_ref, v_ref, kc_ref, vc_ref, bias_ref, o_ref, *, rows, wr):
    r = pl.program_id(1)
    rs = jnp.clip(r - wr // 2, 0, rows - wr)
    start = pl.multiple_of(rs * GRID_W, GRID_W)
    nloc = wr * GRID_W
    scale = D_C ** -0.5
    for h in range(H_C):
        cs = slice(h * D_C, (h + 1) * D_C)
        q = q_ref[:, cs]
        kw = k_ref[pl.ds(start, nloc), cs]
        vw = v_ref[pl.ds(start, nloc), cs]
        s_loc = _dot_nt(q, kw) * scale + bias_ref[0, h]
        s_ctx = _dot_nt(q, kc_ref[:, cs]) * scale
        o = _softmax_pv([(s_loc, vw), (s_ctx, vc_ref[:, cs])])
        o_ref[:, cs] = o.astype(BF16)


def _na_bias_table(rpb, rows, wr):
    col = jnp.arange(GRID_W)
    cs = jnp.clip(col - WIN_C // 2, 0, GRID_W - WIN_C)
    col_mask = (col[None, :] >= cs[:, None]) & (col[None, :] < cs[:, None] + WIN_C)
    dc_idx = jnp.clip(col[None, :] - col[:, None] + WIN_C - 1, 0, 2 * WIN_C - 2)
    tabs = []
    for off in range(wr):
        dr_idx = jnp.arange(wr) - off + (WIN_R - 1)
        bias = rpb.astype(F32)[:, dr_idx[:, None, None], dc_idx[None, :, :]]
        bias = jnp.where(col_mask[None, None], bias, NEG)
        tabs.append(bias.transpose(0, 2, 1, 3).reshape(H_C, GRID_W, wr * GRID_W))
    return jnp.stack(tabs)


def _na_attn(q, k, v, cache_k, cache_v, rpb, B, T):
    n = q.shape[0]
    rows = T // GRID_W
    wr = min(WIN_R, rows)
    Lc = cache_k.shape[0] // B
    bias = _na_bias_table(rpb, rows, wr)
    W = W_HEADS

    def bias_idx(b, r):
        return (r - jnp.clip(r - wr // 2, 0, rows - wr), 0, 0, 0)

    return pl.pallas_call(
        functools.partial(_na_kernel, rows=rows, wr=wr),
        grid=(B, rows),
        in_specs=[pl.BlockSpec((GRID_W, W), lambda b, r: (b * rows + r, 0)),
                  pl.BlockSpec((T, W), lambda b, r: (b, 0)),
                  pl.BlockSpec((T, W), lambda b, r: (b, 0)),
                  pl.BlockSpec((Lc, W), lambda b, r: (b, 0)),
                  pl.BlockSpec((Lc, W), lambda b, r: (b, 0)),
                  pl.BlockSpec((1, H_C, GRID_W, wr * GRID_W), bias_idx)],
        out_specs=pl.BlockSpec((GRID_W, W), lambda b, r: (b * rows + r, 0)),
        out_shape=jax.ShapeDtypeStruct((n, W), BF16),
        compiler_params=_cparams("parallel", "arbitrary"), name="na_attn",
    )(q, k, v, cache_k, cache_v, bias)


def _dense_attn_kernel(q_ref, k_ref, v_ref, o_ref):
    scale = D_C ** -0.5
    for h in range(H_C):
        cs = slice(h * D_C, (h + 1) * D_C)
        s = _dot_nt(q_ref[:, cs], k_ref[:, cs]) * scale
        o_ref[:, cs] = _softmax_pv([(s, v_ref[:, cs])]).astype(BF16)


def _dense_attn(q, k, v, B, T):
    n = q.shape[0]
    tq = _pick_tile(T, 256)
    nq = T // tq
    W = W_HEADS
    return pl.pallas_call(
        _dense_attn_kernel, grid=(B, nq),
        in_specs=[pl.BlockSpec((tq, W), lambda b, i: (b * nq + i, 0)),
                  pl.BlockSpec((T, W), lambda b, i: (b, 0)),
                  pl.BlockSpec((T, W), lambda b, i: (b, 0))],
        out_specs=pl.BlockSpec((tq, W), lambda b, i: (b * nq + i, 0)),
        out_shape=jax.ShapeDtypeStruct((n, W), BF16),
        compiler_params=_cparams("parallel", "parallel"), name="dense_attn",
    )(q, k, v)


HALO = 8
DN_ROWS = H_B * CHUNK
INV_BASE = 8


def _split_bf16(a):
    hi = a.astype(BF16)
    return hi, (a - hi.astype(F32)).astype(BF16)


def _mm3(a, b):
    ah, al = _split_bf16(a)
    bh, bl = _split_bf16(b)
    return _dot(ah, bh) + (_dot(ah, bl) + _dot(al, bh))


def _dn_prep_kernel(x_ref, xp_ref, xn_ref, ab_ref, abT_ref, cw_ref, alr_ref, dtr_ref, alc_ref, dtc_ref,
                    u_ref, wq_ref, kg_ref, a_ref, eg_ref, xs_ref):
    c = pl.program_id(1)
    nc = pl.num_programs(1)
    W = W_HEADS
    pad = CONV_K // 2
    xs_ref[HALO:HALO + CHUNK, :] = x_ref[...]
    xs_ref[0:HALO, :] = jnp.where(c > 0, xp_ref[...], 0.0)
    xs_ref[HALO + CHUNK:, :] = jnp.where(c < nc - 1, xn_ref[...], 0.0)
    y = None
    for j in range(CONV_K):
        t = cw_ref[j:j + 1, :] * xs_ref[HALO - pad + j:HALO - pad + j + CHUNK, :]
        y = t if y is None else y + t
    y = _silu(y)

    def l2n(v):
        return v * lax.rsqrt(jnp.sum(v * v, axis=-1, keepdims=True) + EPS)

    qs, ks, vs = [], [], []
    for h in range(H_B):
        qs.append(l2n(y[:, h * D_B:(h + 1) * D_B]) * (D_B ** -0.5))
        ks.append(l2n(y[:, W + h * D_B:W + (h + 1) * D_B]))
        vs.append(y[:, 2 * W + h * D_B:2 * W + (h + 1) * D_B])
    qcat = jnp.concatenate(qs, axis=0)
    kcat = jnp.concatenate(ks, axis=0)
    vcat = jnp.concatenate(vs, axis=0)
    kb16 = kcat.astype(BF16)
    kk = _dot_nt(kb16, kb16)
    qk = _dot_nt(qcat.astype(BF16), kb16)

    nh2 = 2 * H_B
    ab = ab_ref[...]
    g_col = -jnp.exp(alr_ref[...]) * _softplus(ab[:, 0:nh2] + dtr_ref[...])
    beta_col = _sigmoid(ab[:, nh2:2 * nh2])
    g_row = -jnp.exp(alc_ref[...]) * _softplus(abT_ref[0, 0:nh2, :] + dtc_ref[...])
    ri = _iota((CHUNK, CHUNK), 0)
    ci = _iota((CHUNK, CHUNK), 1)
    lo = (ri >= ci).astype(F32)
    up = (ri <= ci).astype(F32)
    gc = jnp.where(_iota((CHUNK, nh2), 1) < H_B, _dot_f32(lo, g_col), _dot_f32(up, g_col))
    gr = jnp.where(_iota((nh2, CHUNK), 0) < H_B, _dot_f32(g_row, up), _dot_f32(g_row, lo))
    glast = jnp.sum(g_col, axis=0, keepdims=True)
    eg_ref[0] = jnp.exp(jnp.sum(g_row, axis=1, keepdims=True)) * jnp.ones((nh2, LANES), F32)

    R = DN_ROWS
    rr = _iota((R, R), 0)
    cc = _iota((R, R), 1)
    same = (rr // CHUNK) == (cc // CHUNK)
    eye = (rr == cc).astype(F32)
    for d in range(2):
        cols = [d * H_B + h for h in range(H_B)]
        gc_cat = jnp.concatenate([gc[:, j:j + 1] for j in cols], axis=0)
        beta_cat = jnp.concatenate([beta_col[:, j:j + 1] for j in cols], axis=0)
        gl_cat = jnp.concatenate([jnp.zeros((CHUNK, 1), F32) + glast[:, j:j + 1] for j in cols], axis=0)
        gr_cat = jnp.concatenate([gr[j:j + 1, :] for j in cols], axis=1)
        tri = same & ((rr >= cc) if d == 0 else (rr <= cc))
        strict = same & ((rr > cc) if d == 0 else (rr < cc))
        decay = jnp.where(tri, jnp.exp(jnp.where(tri, gc_cat - gr_cat, 0.0)), 0.0)
        L = jnp.where(strict, beta_cat * kk * decay, 0.0)
        a_in = jnp.where(tri, qk * decay, 0.0)
        inb = (rr // INV_BASE) == (cc // INV_BASE)
        Dg = jnp.where(inb, L, 0.0)
        X = eye - Dg
        P = _mm3(Dg, Dg)
        X = X + _mm3(X, P)
        X = X + _mm3(X, _mm3(P, P))
        bs = INV_BASE
        while bs < CHUNK:
            outer = (rr // (2 * bs)) == (cc // (2 * bs))
            C = jnp.where(outer & jnp.logical_not(inb), L, 0.0)
            X = X - _mm3(_mm3(X, C), X)
            inb = outer
            bs *= 2
        Xb = X.astype(BF16)
        egc = jnp.exp(gc_cat)
        u_ref[d, 0] = _dot(Xb, (vcat * beta_cat).astype(BF16))
        w = _dot(Xb, (kcat * (beta_cat * egc)).astype(BF16))
        qg = qcat * egc
        for h in range(H_B):
            wq_ref[d, 0, h, 0:CHUNK, :] = w[h * CHUNK:(h + 1) * CHUNK].astype(BF16)
            wq_ref[d, 0, h, CHUNK:, :] = qg[h * CHUNK:(h + 1) * CHUNK].astype(BF16)
        kg_ref[d, 0] = (kcat * jnp.exp(gl_cat - gc_cat)).astype(BF16)
        a_ref[d, 0] = a_in.astype(BF16)


def _dn_prep(xb, ab, abT, conv_w, a_log, dt_bias, B, T):
    n = xb.shape[0]
    nc = T // CHUNK
    nblk = n // HALO
    cpb = CHUNK // HALO
    W3 = xb.shape[1]
    nh2 = 2 * H_B
    abT3 = abT.reshape(16, n // CHUNK, CHUNK).transpose(1, 0, 2)
    al = a_log.reshape(1, nh2)
    dt = dt_bias.reshape(1, nh2)
    R = DN_ROWS
    bc = lambda b, c: b * nc + c
    const = lambda b, c: (0, 0)
    return pl.pallas_call(
        _dn_prep_kernel, grid=(B, nc),
        in_specs=[pl.BlockSpec((CHUNK, W3), lambda b, c: (bc(b, c), 0)),
                  pl.BlockSpec((HALO, W3), lambda b, c: (jnp.maximum(bc(b, c) * cpb - 1, 0), 0)),
                  pl.BlockSpec((HALO, W3), lambda b, c: (jnp.minimum((bc(b, c) + 1) * cpb, nblk - 1), 0)),
                  pl.BlockSpec((CHUNK, 16), lambda b, c: (bc(b, c), 0)),
                  pl.BlockSpec((1, 16, CHUNK), lambda b, c: (bc(b, c), 0, 0)),
                  pl.BlockSpec(conv_w.shape, const),
                  pl.BlockSpec((1, nh2), const), pl.BlockSpec((1, nh2), const),
                  pl.BlockSpec((nh2, 1), const), pl.BlockSpec((nh2, 1), const)],
        out_specs=[pl.BlockSpec((2, 1, R, D_B), lambda b, c: (0, bc(b, c), 0, 0)),
                   pl.BlockSpec((2, 1, H_B, 2 * CHUNK, D_B), lambda b, c: (0, bc(b, c), 0, 0, 0)),
                   pl.BlockSpec((2, 1, R, D_B), lambda b, c: (0, bc(b, c), 0, 0)),
                   pl.BlockSpec((2, 1, R, R), lambda b, c: (0, bc(b, c), 0, 0)),
                   pl.BlockSpec((1, nh2, LANES), lambda b, c: (bc(b, c), 0, 0))],
        out_shape=[jax.ShapeDtypeStruct((2, B * nc, R, D_B), F32),
                   jax.ShapeDtypeStruct((2, B * nc, H_B, 2 * CHUNK, D_B), BF16),
                   jax.ShapeDtypeStruct((2, B * nc, R, D_B), BF16),
                   jax.ShapeDtypeStruct((2, B * nc, R, R), BF16),
                   jax.ShapeDtypeStruct((B * nc, nh2, LANES), F32)],
        scratch_shapes=[pltpu.VMEM((CHUNK + 2 * HALO, W3), F32)],
        compiler_params=_cparams("parallel", "parallel"), name="dn_prep",
    )(xb, xb, xb, ab, abT3, conv_w, al, dt, al.reshape(nh2, 1), dt.reshape(nh2, 1))


def _dn_scan_kernel(*refs):
    s0_ref = refs[0]
    per_dir = (refs[1:6], refs[6:11])
    of_ref, ob_ref, sfin_ref, s_ref = refs[11:15]
    c = pl.program_id(1)
    nc = pl.num_programs(1)

    @pl.when(c == 0)
    def _():
        s_ref[...] = s0_ref[0]

    for d, (u_ref, wq_ref, kg_ref, a_ref, eg_ref) in enumerate(per_dir):
        o_ref = of_ref if d == 0 else ob_ref
        vns, qss = [], []
        for h in range(H_B):
            sb = s_ref[d, h].astype(BF16)
            r = _dot(wq_ref[0, 0, h], sb)
            vns.append((u_ref[0, 0, h * CHUNK:(h + 1) * CHUNK, :] - r[0:CHUNK]).astype(BF16))
            qss.append(r[CHUNK:])
        vn = jnp.concatenate(vns, axis=0)
        av = _dot(a_ref[0, 0], vn)
        for h in range(H_B):
            rows = slice(h * CHUNK, (h + 1) * CHUNK)
            o_ref[:, h * D_B:(h + 1) * D_B] = qss[h] + av[rows]
            eg = eg_ref[0, d * H_B + h:d * H_B + h + 1, :]
            s_ref[d, h] = s_ref[d, h] * eg + _dot_tn(kg_ref[0, 0, rows, :], vns[h])

    @pl.when(c == nc - 1)
    def _():
        sfin_ref[0] = s_ref[...]


def _dn_scan(prep, s0, B, T):
    u, wq, kg, a, eg = prep
    nc = T // CHUNK
    R = DN_ROWS
    nh2 = 2 * H_B

    def dir_specs(d):
        cb = (lambda b, c: b * nc + c) if d == 0 else (lambda b, c: b * nc + nc - 1 - c)
        return [pl.BlockSpec((1, 1, R, D_B), lambda b, c: (d, cb(b, c), 0, 0)),
                pl.BlockSpec((1, 1, H_B, 2 * CHUNK, D_B), lambda b, c: (d, cb(b, c), 0, 0, 0)),
                pl.BlockSpec((1, 1, R, D_B), lambda b, c: (d, cb(b, c), 0, 0)),
                pl.BlockSpec((1, 1, R, R), lambda b, c: (d, cb(b, c), 0, 0)),
                pl.BlockSpec((1, nh2, LANES), lambda b, c: (cb(b, c), 0, 0))]

    st = (1, 2, H_B, D_B, D_B)
    st_spec = pl.BlockSpec(st, lambda b, c: (b, 0, 0, 0, 0))
    return pl.pallas_call(
        _dn_scan_kernel, grid=(B, nc),
        in_specs=[st_spec] + dir_specs(0) + dir_specs(1),
        out_specs=[pl.BlockSpec((CHUNK, W_HEADS), lambda b, c: (b * nc + c, 0)),
                   pl.BlockSpec((CHUNK, W_HEADS), lambda b, c: (b * nc + nc - 1 - c, 0)),
                   st_spec],
        out_shape=[jax.ShapeDtypeStruct((B * T, W_HEADS), F32), jax.ShapeDtypeStruct((B * T, W_HEADS), F32),
                   jax.ShapeDtypeStruct((B,) + st[1:], F32)],
        scratch_shapes=[pltpu.VMEM(st[1:], F32)],
        compiler_params=_cparams("parallel", "arbitrary"), name="dn_scan",
    )(s0, u, wq, kg, a, eg, u, wq, kg, a, eg)


def _merge_kernel(x_ref, mod_ref, g1_ref, g2_ref, oa_ref, of_ref, ob_ref, zb_ref, gon_ref, oc_ref,
                  wg_ref, wa_ref, wb_ref, wc_ref, wo_ref, wrT_ref, x1_ref, h2_ref, lgT_ref):
    D = x_ref.shape[1]
    x = x_ref[...]
    sh1, sc1, gt1, sh2, sc2 = [mod_ref[0, j:j + 1, :] for j in range(5)]
    hb = (_rms(x, g1_ref[...]) * (1.0 + sc1) + sh1).astype(BF16)
    od = of_ref[...] + ob_ref[...]
    z = zb_ref[...]
    gon = gon_ref[...]
    obs = []
    for h in range(H_B):
        cs = slice(h * D_B, (h + 1) * D_B)
        obs.append(_rms(od[:, cs], gon) * _silu(z[:, cs]))
    ob = jnp.concatenate(obs, axis=1).astype(BF16)

    def gate(j):
        return _sigmoid(_dot(hb, wg_ref[:, j * D:(j + 1) * D]))

    merged = (gate(0) * _dot(oa_ref[...], wa_ref[...])
              + gate(1) * _dot(ob, wb_ref[...])
              + gate(2) * _dot(oc_ref[...], wc_ref[...]))
    x1 = x + gt1 * _dot(merged.astype(BF16), wo_ref[...])
    x1_ref[...] = x1
    h2 = _rms(x1, g2_ref[...]) * (1.0 + sc2) + sh2
    h2_ref[...] = h2
    lgT_ref[...] = lax.dot_general(wrT_ref[...], h2, (((1,), (1,)), ((), ())), preferred_element_type=F32,
                                   precision=lax.Precision.HIGHEST)


def _merge(x, mod5, g1, g2, oa, o_f, o_b, zb, g_onorm, oc, w_g, w_a, w_b, w_c, w_o, w_rT, T):
    n, D = x.shape
    tm = _pick_tile(T, 256)
    nt = T // tm
    W = W_HEADS
    E = w_rT.shape[0]
    row = lambda i: (i, 0)
    const = lambda i: (0, 0)
    full = lambda a: pl.BlockSpec(a.shape, const)
    return pl.pallas_call(
        _merge_kernel, grid=(n // tm,),
        in_specs=[pl.BlockSpec((tm, D), row), pl.BlockSpec((1, 8, D), lambda i: (i // nt, 0, 0)),
                  pl.BlockSpec((1, D), const), pl.BlockSpec((1, D), const),
                  pl.BlockSpec((tm, W), row), pl.BlockSpec((tm, W), row), pl.BlockSpec((tm, W), row),
                  pl.BlockSpec((tm, W), row), pl.BlockSpec((1, D_B), const), pl.BlockSpec((tm, W), row),
                  full(w_g), full(w_a), full(w_b), full(w_c), full(w_o), full(w_rT)],
        out_specs=[pl.BlockSpec((tm, D), row), pl.BlockSpec((tm, D), row), pl.BlockSpec((E, tm), lambda i: (0, i))],
        out_shape=[jax.ShapeDtypeStruct((n, D), F32), jax.ShapeDtypeStruct((n, D), F32),
                   jax.ShapeDtypeStruct((E, n), F32)],
        compiler_params=_cparams("parallel"), name="merge",
    )(x, mod5, g1, g2, oa, o_f, o_b, zb, g_onorm, oc, w_g, w_a, w_b, w_c, w_o, w_rT)


def _route_kernel(lg_ref, idx_ref, sel_ref, w_ref, *, cap):
    E, R, _ = lg_ref.shape
    lg = lg_ref[...]
    m = jnp.max(lg, axis=0, keepdims=True)
    ex = jnp.exp(lg - m)
    aff = ex / jnp.sum(ex, axis=0, keepdims=True)
    bits = pltpu.bitcast(aff, I32)

    def count(mask):
        t = jnp.sum(jnp.where(mask, 1.0, 0.0), axis=2, keepdims=True)
        return jnp.sum(t, axis=1, keepdims=True)

    def bisect(_, carry):
        lo, hi = carry
        mid = lo + lax.shift_right_logical(hi - lo, 1)
        ge = count(bits >= mid) >= cap
        return jnp.where(ge, mid, lo), jnp.where(ge, hi, mid)

    lo0 = jnp.zeros((E, 1, 1), I32)
    hi0 = jnp.full((E, 1, 1), 0x7F800000, I32)
    thr, _ = lax.fori_loop(0, 31, bisect, (lo0, hi0))
    gt = bits > thr
    eq = bits == thr
    need = cap - count(gt)

    lane_r = _iota((LANES, LANES), 0)
    lane_c = _iota((LANES, LANES), 1)
    u_lane = (lane_r <= lane_c).astype(BF16)
    row_r = _iota((R, R), 0)
    row_c = _iota((R, R), 1)
    sl_row = (row_c < row_r).astype(BF16)
    u_row = (row_r <= row_c).astype(BF16)
    ones8 = jnp.ones((8, LANES), BF16)
    eye_l = (lane_r == lane_c).astype(BF16)

    def prefix(mask_f32):
        w = _dot(mask_f32.astype(BF16), u_lane)
        tot = (jnp.zeros((R, LANES), F32) + w[:, LANES - 1:LANES]).astype(BF16)
        return w + _dot(sl_row, tot), w

    gtf = jnp.where(gt, 1.0, 0.0)
    eqf = jnp.where(eq, 1.0, 0.0)
    for e in range(E):
        gt_e = gtf[e]
        eq_e = eqf[e]
        pe, _ = prefix(eq_e)
        sel_e = jnp.maximum(gt_e, jnp.where(pe <= need[e], eq_e, 0.0))
        sel_ref[e] = sel_e
        _, w_e = prefix(sel_e)
        w_ref[e] = w_e

    nblk = cap // LANES
    s_base = _iota((LANES, 1), 0).astype(F32)

    def slot_block(i, _):
        e = i // nblk
        jb = i % nblk
        sel_e = sel_ref[e]
        w_e = w_ref[e]
        s_col = s_base + jnp.asarray(jb * LANES, F32)
        rt_row = _dot_nt(ones8, sel_e.astype(BF16))
        rc_row = _dot(rt_row.astype(BF16), u_row)[0:1, :]
        below = rc_row <= s_col
        rho = jnp.sum(jnp.where(below, 1.0, 0.0), axis=1, keepdims=True)
        before = jnp.max(jnp.where(below, rc_row, 0.0), axis=1, keepdims=True)
        onehot = (_iota((LANES, R), 1).astype(F32) == rho).astype(BF16)
        wg = _dot(onehot, w_e.astype(BF16))
        lam = jnp.sum(jnp.where(wg <= s_col - before, 1.0, 0.0), axis=1, keepdims=True)
        rho_l = _dot_nt(ones8, (eye_l * rho).astype(BF16))
        lam_l = _dot_nt(ones8, (eye_l * lam).astype(BF16))
        idx_ref[i] = (rho_l[0:1, :] * LANES + lam_l[0:1, :]).astype(I32)
        return 0

    lax.fori_loop(0, E * nblk, slot_block, 0)


def _route(logits_t, cap):
    E, n = logits_t.shape
    R = n // LANES
    nblk = cap // LANES
    idx = pl.pallas_call(
        functools.partial(_route_kernel, cap=cap),
        out_shape=jax.ShapeDtypeStruct((E * nblk, 1, LANES), I32),
        scratch_shapes=[pltpu.VMEM((E, R, LANES), F32), pltpu.VMEM((E, R, LANES), F32)],
        compiler_params=pltpu.CompilerParams(vmem_limit_bytes=VMEM_LIMIT_BYTES), name="route",
    )(logits_t.reshape(E, R, LANES))
    return idx.reshape(E, cap)


def _moe_kernel(idx_ref, idxn_ref, h_hbm, wr_ref, wg_ref, wu_ref, wd_ref, acc_in, acc_hbm,
                xbuf, mbuf, sem_x, sem_m, sem_s, *, nt):
    del acc_in
    e = pl.program_id(0)
    k = pl.program_id(1)
    total = pl.num_programs(0) * nt
    step = e * nt + k
    slot = step % 2
    tm = xbuf.shape[1]

    def row_copies(src_of, dst_of, sem, ids):
        def start(r, _):
            t = ids[0, 0, r]
            pltpu.make_async_copy(src_of(t, r), dst_of(t, r), sem).start()
            return 0
        lax.fori_loop(0, tm, start, 0, unroll=8)

    def wait_rows(sem, buf):
        def wait(r, _):
            pltpu.make_async_copy(buf.at[pl.ds(0, 1)], buf.at[pl.ds(0, 1)], sem).wait()
            return 0
        lax.fori_loop(0, tm, wait, 0, unroll=8)

    def gather_x(ids, s):
        row_copies(lambda t, r: h_hbm.at[pl.ds(t, 1)], lambda t, r: xbuf.at[s, pl.ds(r, 1)], sem_x.at[s], ids)

    @pl.when(step == 0)
    def _():
        gather_x(idx_ref, 0)

    wait_rows(sem_x.at[slot], xbuf.at[slot])

    @pl.when(step + 1 < total)
    def _():
        gather_x(idxn_ref, 1 - slot)

    @pl.when(step > 0)
    def _():
        wait_rows(sem_s, mbuf)
    row_copies(lambda t, r: acc_hbm.at[pl.ds(t, 1)], lambda t, r: mbuf.at[pl.ds(r, 1)], sem_m, idx_ref)

    xe = xbuf[slot].astype(BF16)
    hid = _silu(_dot(xe, wg_ref[0])) * _dot(xe, wu_ref[0])
    lg = _dot(xe, wr_ref[...])
    lane = _iota(lg.shape, 1)
    lg = jnp.where(lane < N_EXPERTS, lg, NEG)
    ex = jnp.exp(lg - jnp.max(lg, axis=-1, keepdims=True))
    aff = ex / jnp.sum(ex, axis=-1, keepdims=True)
    gate = jnp.sum(jnp.where(lane == e, aff, 0.0), axis=-1, keepdims=True)
    ye = _dot(hid.astype(BF16), wd_ref[0]) * gate

    wait_rows(sem_m, mbuf)
    mbuf[...] = mbuf[...] + ye
    row_copies(lambda t, r: mbuf.at[pl.ds(r, 1)], lambda t, r: acc_hbm.at[pl.ds(t, 1)], sem_s, idx_ref)

    @pl.when(step == total - 1)
    def _():
        wait_rows(sem_s, mbuf)


def _moe(h2, idx, w_router_pad, w_gate, w_up, w_down, acc):
    n, D = h2.shape
    E, cap = idx.shape
    F = w_gate.shape[-1]
    tm = _pick_tile(cap, 256)
    nt = cap // tm
    idx2 = idx.reshape(E * nt, 1, tm)
    last = E * nt - 1
    smem_blk = lambda f: pl.BlockSpec((1, 1, tm), f, memory_space=pltpu.SMEM)
    return pl.pallas_call(
        functools.partial(_moe_kernel, nt=nt), grid=(E, nt),
        in_specs=[smem_blk(lambda e, k: (e * nt + k, 0, 0)),
                  smem_blk(lambda e, k: (jnp.minimum(e * nt + k + 1, last), 0, 0)),
                  pl.BlockSpec(memory_space=pl.ANY),
                  pl.BlockSpec(w_router_pad.shape, lambda e, k: (0, 0)),
                  pl.BlockSpec((1, D, F), lambda e, k: (e, 0, 0)),
                  pl.BlockSpec((1, D, F), lambda e, k: (e, 0, 0)),
                  pl.BlockSpec((1, F, D), lambda e, k: (e, 0, 0)),
                  pl.BlockSpec(memory_space=pl.ANY)],
        out_specs=pl.BlockSpec(memory_space=pl.ANY),
        out_shape=jax.ShapeDtypeStruct((n, D), F32),
        input_output_aliases={7: 0},
        scratch_shapes=[pltpu.VMEM((2, tm, D), F32), pltpu.VMEM((tm, D), F32),
                        pltpu.SemaphoreType.DMA((2,)), pltpu.SemaphoreType.DMA(()), pltpu.SemaphoreType.DMA(())],
        compiler_params=_cparams("arbitrary", "arbitrary"), name="moe",
    )(idx2, idx2, h2, w_router_pad, w_gate, w_up, w_down, acc)


def _resid_kernel(x_ref, m_ref, gt_ref, g_ref, o_ref, *, final):
    x2 = x_ref[...] + gt_ref[0] * m_ref[...]
    o_ref[...] = _rms(x2, g_ref[...]) if final else x2


def _resid(x1, m, gt2, g_final, T, final):
    n, D = x1.shape
    tm = _pick_tile(T, 512)
    nt = T // tm
    row = lambda i: (i, 0)
    return pl.pallas_call(
        functools.partial(_resid_kernel, final=final), grid=(n // tm,),
        in_specs=[pl.BlockSpec((tm, D), row), pl.BlockSpec((tm, D), row),
                  pl.BlockSpec((1, 1, D), lambda i: (i // nt, 0, 0)), pl.BlockSpec((1, D), lambda i: (0, 0))],
        out_specs=pl.BlockSpec((tm, D), row), out_shape=jax.ShapeDtypeStruct((n, D), F32),
        compiler_params=_cparams("parallel"), name="resid",
    )(x1, m, gt2, g_final)


def _lambda_init(layer):
    return 0.8 - 0.6 * math.exp(-0.3 * layer)


def _prep_weights(p, l):
    W = W_HEADS
    D = p['w_in'].shape[1]
    w_in = p['w_in'][l]
    nab = 4 * H_B
    o_ab = 3 * W + 4 * W
    o_c = o_ab + nab
    o_g = o_c + 3 * W
    w_main = jnp.concatenate([w_in[:, :o_ab], w_in[:, o_c:o_g]], axis=1).astype(BF16)
    w_ab = w_in[:, o_ab:o_c]
    E = p['w_router'].shape[-1]
    w_r = p['w_router'][l]
    return dict(
        w_main=w_main, w_ab=w_ab, w_abT=w_ab.T,
        w_g=w_in[:, o_g:].astype(BF16),
        w_a=p['w_br_a'][l].astype(BF16), w_b=p['w_br_b'][l].astype(BF16), w_c=p['w_br_c'][l].astype(BF16),
        w_o=p['w_out'][l].astype(BF16),
        w_rT=w_r.T,
        w_r_pad=jnp.pad(w_r, ((0, 0), (0, LANES - E))).astype(BF16),
        w_gate=p['w_e_gate'][l].astype(BF16), w_up=p['w_e_up'][l].astype(BF16), w_down=p['w_e_down'][l].astype(BF16),
        g1=p['g_norm1'][l][None], g2=p['g_norm2'][l][None],
        conv_w=p['conv_w'][l], a_log=p['a_log'][l], dt_bias=p['dt_bias'][l],
        g_onorm=p['g_onorm'][l][None], lam_qk=p['lam_qk'][l], g_subln=p['g_subln'][l][None], rpb=p['rpb'][l],
    )


def _trunk_layer(x, mod, lw, l, B, T, g_final, final, ctx=None):
    n, D = x.shape
    lam_init = _lambda_init(l)
    latent = ctx is not None
    sh1, sc1 = mod[:, 0:1], mod[:, 1:2]
    outs = _inproj(x, sc1, sh1, lw['g1'], lw['w_main'], lw['w_ab'], lw['w_abT'], T,
                   _rope_tables(T) if latent else None, emit_kv=not latent)
    qa, ka, va, xb, zb, qc, kc, vc, ab, abT = outs[:10]
    prep = _dn_prep(xb, ab, abT, lw['conv_w'], lw['a_log'], lw['dt_bias'], B, T)
    if latent:
        ck_a, cv_a, ck_c, cv_c, c_state = ctx
        oa = _diff_attn(qa, ka, va, ck_a, cv_a, lw['lam_qk'], lw['g_subln'], B, T, lam_init)
        o_f, o_b, _ = _dn_scan(prep, c_state, B, T)
        oc = _na_attn(qc, kc, vc, ck_c, cv_c, lw['rpb'], B, T)
        new_ctx = None
    else:
        oa = _diff_attn(qa, ka, va, None, None, lw['lam_qk'], lw['g_subln'], B, T, lam_init)
        s0 = jnp.zeros((B, 2, H_B, D_B, D_B), F32)
        o_f, o_b, s_fin = _dn_scan(prep, s0, B, T)
        oc = _dense_attn(qc, kc, vc, B, T)
        new_ctx = tuple(a.reshape(B, T, 4, 128) for a in outs[10:14]) + (s_fin,)
    mod5 = jnp.pad(mod[:, :5], ((0, 0), (0, 3), (0, 0)))
    x1, h2, lg_t = _merge(x, mod5, lw['g1'], lw['g2'], oa, o_f, o_b, zb, lw['g_onorm'], oc,
                          lw['w_g'], lw['w_a'], lw['w_b'], lw['w_c'], lw['w_o'], lw['w_rT'], T)
    cap = EC_FACTOR * n // N_EXPERTS
    idx = _route(lg_t, cap)
    m = _moe(h2, idx, lw['w_r_pad'], lw['w_gate'], lw['w_up'], lw['w_down'], jnp.zeros((n, D), F32))
    x2 = _resid(x1, m, mod[:, 5:6], g_final, T, final)
    return x2, new_ctx


def kernel(x_prompt, x_sample, c, cache_diff_k, cache_diff_v, cache_na_k, cache_na_v, state_delta, c_ctx, w_ada, b_ada, g_norm1, g_norm2, w_in, conv_w, a_log, dt_bias, g_onorm, lam_qk, g_subln, rpb, w_br_a, w_br_b, w_br_c, w_out, w_router, w_e_gate, w_e_up, w_e_down, g_final):
    p = dict(w_in=w_in, conv_w=conv_w, a_log=a_log, dt_bias=dt_bias, g_onorm=g_onorm, lam_qk=lam_qk,
             g_subln=g_subln, rpb=rpb, w_br_a=w_br_a, w_br_b=w_br_b, w_br_c=w_br_c, w_out=w_out,
             w_router=w_router, w_e_gate=w_e_gate, w_e_up=w_e_up, w_e_down=w_e_down,
             g_norm1=g_norm1, g_norm2=g_norm2)
    depth = w_in.shape[0]
    Bp, Tp, D = x_prompt.shape
    Bs, Ts, _ = x_sample.shape
    gf = g_final[None]

    nrow = 1 + Bs
    cond = jnp.pad(jnp.concatenate([c_ctx[None], c], axis=0), ((0, -nrow % 8), (0, 0)))
    mod = _adaln(cond, w_ada, b_ada).reshape(depth, -1, 6, D)
    layers = [_prep_weights(p, l) for l in range(depth)]

    xp = x_prompt.reshape(Bp * Tp, D)
    new_ctx = []
    for l in range(depth):
        mod_ctx = jnp.broadcast_to(mod[l, 0:1], (Bp, 6, D))
        xp, nc = _trunk_layer(xp, mod_ctx, layers[l], l, Bp, Tp, gf, l == depth - 1)
        new_ctx.append(nc)

    xs = x_sample.reshape(Bs * Ts, D)
    for l in range(depth):
        Lc = cache_diff_k.shape[2]
        flat = lambda a: a[:, l].reshape(Bs * Lc, W_HEADS).astype(BF16)
        ctx = (flat(cache_diff_k), flat(cache_diff_v), flat(cache_na_k), flat(cache_na_v), state_delta[:, l])
        xs, _ = _trunk_layer(xs, mod[l, 1:1 + Bs], layers[l], l, Bs, Ts, gf, l == depth - 1, ctx)

    stack = lambda j: jnp.stack([nc[j] for nc in new_ctx], axis=1)
    return (xp.reshape(Bp, Tp, D), xs.reshape(Bs, Ts, D), stack(0), stack(1), stack(2), stack(3), stack(4))
```

```python
import functools
import math

import jax
import jax.numpy as jnp
from jax import lax
from jax.experimental import pallas as pl
from jax.experimental.pallas import tpu as pltpu

F32 = jnp.float32
BF16 = jnp.bfloat16
I32 = jnp.int32

H_A, D_A = 4, 64
H_B, D_B = 4, 128
H_C, D_C = 4, 128
CONV_K = 5
CHUNK = 64
GRID_W = 64
WIN_R, WIN_C = 8, 16
N_EXPERTS = 16
EC_FACTOR = 2
ROPE_BASE = 10000.0
EPS = 1e-6
NEG = -1e30
W_HEADS = 512
ATTN_KEY_CHUNK = 512

LANES = 128
VMEM_LIMIT_BYTES = 56 * 1024 * 1024


def _cparams(*sem):
    return pltpu.CompilerParams(dimension_semantics=sem, vmem_limit_bytes=VMEM_LIMIT_BYTES)


def _dot(a, b):
    return jnp.dot(a, b, preferred_element_type=F32)


def _dot_nt(a, b):
    return lax.dot_general(a, b, (((1,), (1,)), ((), ())), preferred_element_type=F32)


def _dot_tn(a, b):
    return lax.dot_general(a, b, (((0,), (0,)), ((), ())), preferred_element_type=F32)


def _dot_f32(a, b):
    return jnp.dot(a, b, preferred_element_type=F32, precision=lax.Precision.HIGHEST)


def _sigmoid(x):
    return 1.0 / (1.0 + jnp.exp(-x))


def _silu(x):
    return x * _sigmoid(x)


def _softplus(x):
    return jnp.maximum(x, 0.0) + jnp.log(1.0 + jnp.exp(-jnp.abs(x)))


def _rms(x, g):
    return x * lax.rsqrt(jnp.mean(x * x, axis=-1, keepdims=True) + EPS) * g


def _iota(shape, dim):
    return lax.broadcasted_iota(I32, shape, dim)


def _pick_tile(n, pref):
    t = min(n, pref)
    while n % t:
        t //= 2
    return t


def _adaln_kernel(c_ref, w_ref, b_ref, o_ref):
    c = c_ref[...]
    s = _silu(c).astype(BF16)
    o_ref[0] = _dot(s, w_ref[0].astype(BF16)) + b_ref[0]


def _adaln(cond, w_ada, b_ada):
    L, D, N = w_ada.shape
    R = cond.shape[0]
    tn = _pick_tile(N, 1536)
    return pl.pallas_call(
        _adaln_kernel,
        grid=(L, N // tn),
        in_specs=[pl.BlockSpec((R, D), lambda l, j: (0, 0)),
                  pl.BlockSpec((1, D, tn), lambda l, j: (l, 0, j)),
                  pl.BlockSpec((1, 1, tn), lambda l, j: (l, 0, j))],
        out_specs=pl.BlockSpec((1, R, tn), lambda l, j: (l, 0, j)),
        out_shape=jax.ShapeDtypeStruct((L, R, N), F32),
        compiler_params=_cparams("parallel", "parallel"),
        name="adaln",
    )(cond, w_ada, b_ada.reshape(L, 1, N))


def _inproj_kernel(*refs, rope, emit_kv):
    it = iter(refs)
    x_ref, sc_ref, sh_ref, g_ref, w_ref, wab_ref, wabT_ref = [next(it) for _ in range(7)]
    if rope:
        cos_ref, sa_ref, sb_ref = [next(it) for _ in range(3)]
    qa_ref, ka_ref, va_ref, xb_ref, zb_ref, qc_ref, kc_ref, vc_ref, ab_ref, abT_ref = [next(it) for _ in range(10)]
    if emit_kv:
        ka32_ref, va32_ref, kc32_ref, vc32_ref = [next(it) for _ in range(4)]

    x = x_ref[...]
    h = _rms(x, g_ref[...]) * (1.0 + sc_ref[0]) + sh_ref[0]
    hb = h.astype(BF16)
    W = W_HEADS

    def proj(j, n=1):
        return _dot(hb, w_ref[:, j * W:(j + n) * W])

    def roped(v):
        if not rope:
            return v
        n = v.shape[-1]
        return v * cos_ref[...] + pltpu.roll(v, n - D_A // 4, 1) * sa_ref[...] + pltpu.roll(v, D_A // 4, 1) * sb_ref[...]

    qa = proj(0)
    qa_ref[...] = (roped(qa) * (D_A ** -0.5)).astype(BF16)
    ka = proj(1)
    ka_ref[...] = roped(ka).astype(BF16)
    va = proj(2)
    va_ref[...] = va.astype(BF16)
    xb_ref[...] = proj(3, 3)
    zb_ref[...] = proj(6)
    qc_ref[...] = proj(7).astype(BF16)
    kc = proj(8)
    kc_ref[...] = kc.astype(BF16)
    vc = proj(9)
    vc_ref[...] = vc.astype(BF16)
    ab_ref[...] = _dot_f32(h, wab_ref[...])
    abT_ref[...] = lax.dot_general(wabT_ref[...], h, (((1,), (1,)), ((), ())), preferred_element_type=F32,
                                   precision=lax.Precision.HIGHEST)
    if emit_kv:
        ka32_ref[...] = ka
        va32_ref[...] = va
        kc32_ref[...] = kc
        vc32_ref[...] = vc


def _inproj(x, sc, sh, g, w_main, w_ab, w_abT, T, rope_tabs, emit_kv):
    n, D = x.shape
    tm = _pick_tile(T, 512)
    nt = T // tm
    W = W_HEADS
    rope = rope_tabs is not None
    row = lambda i: (i, 0)
    per_b = lambda i: (i // nt, 0, 0)
    const = lambda i: (0, 0)
    in_specs = [pl.BlockSpec((tm, D), row),
                pl.BlockSpec((1, 1, D), per_b), pl.BlockSpec((1, 1, D), per_b),
                pl.BlockSpec((1, D), const),
                pl.BlockSpec(w_main.shape, const), pl.BlockSpec(w_ab.shape, const), pl.BlockSpec(w_abT.shape, const)]
    args = [x, sc, sh, g, w_main, w_ab, w_abT]
    if rope:
        in_specs += [pl.BlockSpec((tm, W), lambda i: (i % nt, 0))] * 3
        args += list(rope_tabs)
    bf = lambda: jax.ShapeDtypeStruct((n, W), BF16)
    f32 = lambda w: jax.ShapeDtypeStruct((n, w), F32)
    out_shape = [bf(), bf(), bf(), f32(3 * W), f32(W), bf(), bf(), bf(), f32(16), jax.ShapeDtypeStruct((16, n), F32)]
    out_specs = [pl.BlockSpec((tm, W), row)] * 3 + [pl.BlockSpec((tm, 3 * W), row), pl.BlockSpec((tm, W), row)] \
        + [pl.BlockSpec((tm, W), row)] * 3 + [pl.BlockSpec((tm, 16), row), pl.BlockSpec((16, tm), lambda i: (0, i))]
    if emit_kv:
        out_shape += [f32(W)] * 4
        out_specs += [pl.BlockSpec((tm, W), row)] * 4
    return pl.pallas_call(
        functools.partial(_inproj_kernel, rope=rope, emit_kv=emit_kv),
        grid=(n // tm,), in_specs=in_specs, out_specs=out_specs, out_shape=out_shape,
        compiler_params=_cparams("parallel"), name="inproj",
    )(*args)


def _rope_tables(T):
    quarter = D_A // 4
    inv = 1.0 / (ROPE_BASE ** (jnp.arange(quarter, dtype=F32) / quarter))
    t = jnp.arange(T)
    pos = jnp.stack([t // GRID_W, t % GRID_W], axis=-1).astype(F32)
    ang = pos[:, :, None] * inv
    cos, sin = jnp.cos(ang), jnp.sin(ang)
    cos64 = jnp.concatenate([cos[:, 0], cos[:, 0], cos[:, 1], cos[:, 1]], axis=-1)
    zero = jnp.zeros_like(sin[:, 0])
    sa64 = jnp.concatenate([-sin[:, 0], zero, -sin[:, 1], zero], axis=-1)
    sb64 = jnp.concatenate([zero, sin[:, 0], zero, sin[:, 1]], axis=-1)
    rep = W_HEADS // D_A
    return tuple(jnp.tile(a, (1, rep)) for a in (cos64, sa64, sb64))


def _diff_attn_kernel(*refs, has_cache, lam_init):
    if has_cache:
        q_ref, k_ref, v_ref, kc_ref, vc_ref, lq_ref, g_ref, o_ref = refs
    else:
        q_ref, k_ref, v_ref, lq_ref, g_ref, o_ref = refs
    q = q_ref[...]
    lane = _iota(q.shape, 1)
    zero = jnp.zeros_like(q)
    qs = (jnp.where(lane < D_A, q, zero), jnp.where(lane >= D_A, q, zero))
    lq = lq_ref[...]
    lam = (jnp.exp(jnp.sum(lq[0:1] * lq[1:2], axis=-1, keepdims=True))
           - jnp.exp(jnp.sum(lq[2:3] * lq[3:4], axis=-1, keepdims=True)) + lam_init)
    hd = q.shape[1]

    def with_ones(v):
        one = jnp.where(_iota(v.shape, 1) == 0, 1.0, 0.0).astype(BF16)
        return jnp.concatenate([v, one], axis=1)

    T = k_ref.shape[0]
    tk = _pick_tile(T, ATTN_KEY_CHUNK)
    chunks = [(k_ref[j * tk:(j + 1) * tk, :], with_ones(v_ref[j * tk:(j + 1) * tk, :])) for j in range(T // tk)]
    if has_cache:
        chunks.append((kc_ref[...], with_ones(vc_ref[...])))
    tq = q.shape[0]
    outs = []
    for comp in range(2):
        m = jnp.full((tq, 1), NEG, F32)
        acc = jnp.zeros((tq, 2 * hd), F32)
        for kj, vxj in chunks:
            s = _dot_nt(qs[comp], kj)
            m_new = jnp.maximum(m, jnp.max(s, axis=-1, keepdims=True))
            acc = acc * jnp.exp(m - m_new) + _dot(jnp.exp(s - m_new).astype(BF16), vxj)
            m = m_new
        outs.append(acc[:, :hd] * (1.0 / acc[:, hd:hd + 1]))
    o = outs[0] - lam * outs[1]
    o_ref[...] = (_rms(o, g_ref[...]) * (1.0 - lam_init)).astype(BF16)


def _diff_attn(q, k, v, cache_k, cache_v, lam_qk, g_sub, B, T, lam_init):
    n = q.shape[0]
    tq = _pick_tile(T, 256)
    nq = T // tq
    has_cache = cache_k is not None
    hd = 2 * D_A
    in_specs = [pl.BlockSpec((tq, hd), lambda b, h, i: (b * nq + i, h)),
                pl.BlockSpec((T, hd), lambda b, h, i: (b, h)),
                pl.BlockSpec((T, hd), lambda b, h, i: (b, h))]
    args = [q, k, v]
    if has_cache:
        Lc = cache_k.shape[0] // B
        in_specs += [pl.BlockSpec((Lc, hd), lambda b, h, i: (b, h))] * 2
        args += [cache_k, cache_v]
    in_specs += [pl.BlockSpec(lam_qk.shape, lambda b, h, i: (0, 0)), pl.BlockSpec((1, hd), lambda b, h, i: (0, 0))]
    args += [lam_qk, g_sub]
    return pl.pallas_call(
        functools.partial(_diff_attn_kernel, has_cache=has_cache, lam_init=lam_init),
        grid=(B, H_A, nq), in_specs=in_specs,
        out_specs=pl.BlockSpec((tq, hd), lambda b, h, i: (b * nq + i, h)),
        out_shape=jax.ShapeDtypeStruct((n, W_HEADS), BF16),
        compiler_params=_cparams("parallel", "parallel", "parallel"), name="diff_attn",
    )(*args)


def _softmax_pv(parts):
    m = None
    for s, _ in parts:
        mi = jnp.max(s, axis=-1, keepdims=True)
        m = mi if m is None else jnp.maximum(m, mi)
    ps = [jnp.exp(s - m) for s, _ in parts]
    l = None
    for p in ps:
        li = jnp.sum(p, axis=-1, keepdims=True)
        l = li if l is None else l + li
    r = 1.0 / l
    o = None
    for p, (_, v) in zip(ps, parts):
        oi = _dot((p * r).astype(BF16), v)
        o = oi if o is None else o + oi
    return o


def _na_kernel(q_ref, k_ref, v_ref, kc_ref, vc_ref, bias_ref, o_ref, *, rows, wr):
    r = pl.program_id(1)
    rs = jnp.clip(r - wr // 2, 0, rows - wr)
    start = pl.multiple_of(rs * GRID_W, GRID_W)
    nloc = wr * GRID_W
    scale = D_C ** -0.5
    for h in range(H_C):
        cs = slice(h * D_C, (h + 1) * D_C)
        q = q_ref[:, cs]
        kw = k_ref[pl.ds(start, nloc), cs]
        vw = v_ref[pl.ds(start, nloc), cs]
        s_loc = _dot_nt(q, kw) * scale + bias_ref[0, h]
        s_ctx = _dot_nt(q, kc_ref[:, cs]) * scale
        o = _softmax_pv([(s_loc, vw), (s_ctx, vc_ref[:, cs])])
        o_ref[:, cs] = o.astype(BF16)


def _na_bias_table(rpb, rows, wr):
    col = jnp.arange(GRID_W)
    cs = jnp.clip(col - WIN_C // 2, 0, GRID_W - WIN_C)
    col_mask = (col[None, :] >= cs[:, None]) & (col[None, :] < cs[:, None] + WIN_C)
    dc_idx = jnp.clip(col[None, :] - col[:, None] + WIN_C - 1, 0, 2 * WIN_C - 2)
    onehot = (dc_idx[:, :, None] == jnp.arange(2 * WIN_C - 1)).astype(F32)
    base = jnp.einsum('hdj,qkj->hdqk', rpb.astype(F32), onehot, precision=lax.Precision.HIGHEST)
    base = jnp.where(col_mask[None, None], base, NEG)
    tabs = []
    for off in range(wr):
        lo = WIN_R - 1 - off
        tabs.append(base[:, lo:lo + wr].transpose(0, 2, 1, 3).reshape(H_C, GRID_W, wr * GRID_W))
    return jnp.stack(tabs)


def _na_attn(q, k, v, cache_k, cache_v, rpb, B, T):
    n = q.shape[0]
    rows = T // GRID_W
    wr = min(WIN_R, rows)
    Lc = cache_k.shape[0] // B
    bias = _na_bias_table(rpb, rows, wr)
    W = W_HEADS

    def bias_idx(b, r):
        return (r - jnp.clip(r - wr // 2, 0, rows - wr), 0, 0, 0)

    return pl.pallas_call(
        functools.partial(_na_kernel, rows=rows, wr=wr),
        grid=(B, rows),
        in_specs=[pl.BlockSpec((GRID_W, W), lambda b, r: (b * rows + r, 0)),
                  pl.BlockSpec((T, W), lambda b, r: (b, 0)),
                  pl.BlockSpec((T, W), lambda b, r: (b, 0)),
                  pl.BlockSpec((Lc, W), lambda b, r: (b, 0)),
                  pl.BlockSpec((Lc, W), lambda b, r: (b, 0)),
                  pl.BlockSpec((1, H_C, GRID_W, wr * GRID_W), bias_idx)],
        out_specs=pl.BlockSpec((GRID_W, W), lambda b, r: (b * rows + r, 0)),
        out_shape=jax.ShapeDtypeStruct((n, W), BF16),
        compiler_params=_cparams("parallel", "arbitrary"), name="na_attn",
    )(q, k, v, cache_k, cache_v, bias)


def _dense_attn_kernel(q_ref, k_ref, v_ref, o_ref):
    scale = D_C ** -0.5
    for h in range(H_C):
        cs = slice(h * D_C, (h + 1) * D_C)
        s = _dot_nt(q_ref[:, cs], k_ref[:, cs]) * scale
        o_ref[:, cs] = _softmax_pv([(s, v_ref[:, cs])]).astype(BF16)


def _dense_attn(q, k, v, B, T):
    n = q.shape[0]
    tq = _pick_tile(T, 256)
    nq = T // tq
    W = W_HEADS
    return pl.pallas_call(
        _dense_attn_kernel, grid=(B, nq),
        in_specs=[pl.BlockSpec((tq, W), lambda b, i: (b * nq + i, 0)),
                  pl.BlockSpec((T, W), lambda b, i: (b, 0)),
                  pl.BlockSpec((T, W), lambda b, i: (b, 0))],
        out_specs=pl.BlockSpec((tq, W), lambda b, i: (b * nq + i, 0)),
        out_shape=jax.ShapeDtypeStruct((n, W), BF16),
        compiler_params=_cparams("parallel", "parallel"), name="dense_attn",
    )(q, k, v)


HALO = 8
DN_ROWS = H_B * CHUNK
INV_BASE = 8
DN_CHUNKS_PER_STEP = 2


def _bmm(a, b):
    return lax.dot_general(a, b, (((2,), (1,)), ((0,), (0,))), preferred_element_type=F32)


def _bmm_nt(a, b):
    return lax.dot_general(a, b, (((2,), (2,)), ((0,), (0,))), preferred_element_type=F32)


def _bmmb(a, b):
    return _bmm(a.astype(BF16), b.astype(BF16))


def _dn_prep_kernel(x_ref, xp_ref, xn_ref, ab_ref, abT_ref, cw_ref, alr_ref, dtr_ref, alc_ref, dtc_ref,
                    u_ref, wq_ref, kg_ref, a_ref, eg_ref, xs_ref):
    c = pl.program_id(1)
    nsteps = pl.num_programs(1)
    cps = abT_ref.shape[0]
    rows = cps * CHUNK
    pad = CONV_K // 2
    xs_ref[HALO:HALO + rows, :] = x_ref[...]
    xs_ref[0:HALO, :] = jnp.where(c > 0, xp_ref[...], 0.0)
    xs_ref[HALO + rows:, :] = jnp.where(c < nsteps - 1, xn_ref[...], 0.0)
    y = None
    for j in range(CONV_K):
        t = cw_ref[j:j + 1, :] * xs_ref[HALO - pad + j:HALO - pad + j + rows, :]
        y = t if y is None else y + t
    y = _silu(y)
    parts = []
    for j in range(cps):
        parts += _dn_chunk_terms(y[j * CHUNK:(j + 1) * CHUNK], ab_ref[j * CHUNK:(j + 1) * CHUNK, :], abT_ref[j],
                                 alr_ref, dtr_ref, alc_ref, dtc_ref, eg_ref, j)
    cat = lambda key: jnp.concatenate([p[key] for p in parts], axis=0)
    L = cat('L')
    ri = _iota((CHUNK, CHUNK), 0)
    ci = _iota((CHUNK, CHUNK), 1)
    inb = (ri // INV_BASE) == (ci // INV_BASE)
    Dg = jnp.where(inb, L, 0.0)
    X = (ri == ci).astype(F32) - Dg
    P = _bmmb(Dg, Dg)
    X = X + _bmmb(X, P)
    X = X + _bmmb(X, _bmmb(P, P))
    bs = INV_BASE
    while bs < CHUNK:
        outer = (ri // (2 * bs)) == (ci // (2 * bs))
        C = jnp.where(outer & jnp.logical_not(inb), L, 0.0)
        Xb = X.astype(BF16)
        X = X - _bmm(_bmm(Xb, C.astype(BF16)).astype(BF16), Xb)
        inb = outer
        bs *= 2
    Xb = X.astype(BF16)
    u = _bmm(Xb, cat('vb').astype(BF16))
    w = _bmm(Xb, cat('kbg').astype(BF16))
    for i, p in enumerate(parts):
        d, j = p['d'], p['j']
        rows_i = slice(i * H_B, (i + 1) * H_B)
        u_ref[d, j] = u[rows_i]
        wq_ref[d, j] = jnp.concatenate([w[rows_i], p['qg']], axis=1).astype(BF16)
        kg_ref[d, j] = p['kg'].astype(BF16)
        a_ref[d, j] = p['a_in'].astype(BF16)


def _dn_chunk_terms(y, ab, abT, alr_ref, dtr_ref, alc_ref, dtc_ref, eg_ref, j):
    W = W_HEADS

    def l2n(v):
        return v * lax.rsqrt(jnp.sum(v * v, axis=-1, keepdims=True) + EPS)

    q = jnp.stack([l2n(y[:, h * D_B:(h + 1) * D_B]) * (D_B ** -0.5) for h in range(H_B)])
    k = jnp.stack([l2n(y[:, W + h * D_B:W + (h + 1) * D_B]) for h in range(H_B)])
    v = jnp.stack([y[:, 2 * W + h * D_B:2 * W + (h + 1) * D_B] for h in range(H_B)])
    kb16 = k.astype(BF16)
    kk = _bmm_nt(kb16, kb16)
    qk = _bmm_nt(q.astype(BF16), kb16)

    nh2 = 2 * H_B
    g_col = -jnp.exp(alr_ref[...]) * _softplus(ab[:, 0:nh2] + dtr_ref[...])
    beta_col = _sigmoid(ab[:, nh2:2 * nh2])
    g_row = -jnp.exp(alc_ref[...]) * _softplus(abT[0:nh2, :] + dtc_ref[...])
    ri = _iota((CHUNK, CHUNK), 0)
    ci = _iota((CHUNK, CHUNK), 1)
    lo = (ri >= ci).astype(F32)
    up = (ri <= ci).astype(F32)
    gc = jnp.where(_iota((CHUNK, nh2), 1) < H_B, _dot_f32(lo, g_col), _dot_f32(up, g_col))
    gr = jnp.where(_iota((nh2, CHUNK), 0) < H_B, _dot_f32(g_row, up), _dot_f32(g_row, lo))
    glast = jnp.sum(g_col, axis=0, keepdims=True)
    eg_ref[j] = jnp.exp(jnp.sum(g_row, axis=1, keepdims=True)) * jnp.ones((nh2, LANES), F32)

    terms = []
    for d in range(2):
        cols = [d * H_B + h for h in range(H_B)]
        gc_h = jnp.stack([gc[:, i:i + 1] for i in cols])
        beta_h = jnp.stack([beta_col[:, i:i + 1] for i in cols])
        gl_h = jnp.stack([glast[:, i:i + 1] for i in cols])
        gr_h = jnp.stack([gr[i:i + 1, :] for i in cols])
        tri = (ri >= ci) if d == 0 else (ri <= ci)
        strict = (ri > ci) if d == 0 else (ri < ci)
        decay = jnp.where(tri, jnp.exp(jnp.where(tri, gc_h - gr_h, 0.0)), 0.0)
        egc = jnp.exp(gc_h)
        terms.append(dict(d=d, j=j,
                          L=jnp.where(strict, beta_h * kk * decay, 0.0),
                          a_in=jnp.where(tri, qk * decay, 0.0),
                          vb=v * beta_h, kbg=k * (beta_h * egc), qg=q * egc,
                          kg=k * jnp.exp(gl_h - gc_h)))
    return terms


def _dn_prep(xb, ab, abT, conv_w, a_log, dt_bias, B, T):
    n = xb.shape[0]
    nc = T // CHUNK
    cps = _pick_tile(nc, DN_CHUNKS_PER_STEP)
    ns = nc // cps
    rows = cps * CHUNK
    nblk = n // HALO
    bps = rows // HALO
    W3 = xb.shape[1]
    nh2 = 2 * H_B
    abT3 = abT.reshape(16, n // CHUNK, CHUNK).transpose(1, 0, 2)
    al = a_log.reshape(1, nh2)
    dt = dt_bias.reshape(1, nh2)
    st = lambda b, c: b * ns + c
    idx5 = lambda b, c: (0, st(b, c), 0, 0, 0)
    const = lambda b, c: (0, 0)
    return pl.pallas_call(
        _dn_prep_kernel, grid=(B, ns),
        in_specs=[pl.BlockSpec((rows, W3), lambda b, c: (st(b, c), 0)),
                  pl.BlockSpec((HALO, W3), lambda b, c: (jnp.maximum(st(b, c) * bps - 1, 0), 0)),
                  pl.BlockSpec((HALO, W3), lambda b, c: (jnp.minimum((st(b, c) + 1) * bps, nblk - 1), 0)),
                  pl.BlockSpec((rows, 16), lambda b, c: (st(b, c), 0)),
                  pl.BlockSpec((cps, 16, CHUNK), lambda b, c: (st(b, c), 0, 0)),
                  pl.BlockSpec(conv_w.shape, const),
                  pl.BlockSpec((1, nh2), const), pl.BlockSpec((1, nh2), const),
                  pl.BlockSpec((nh2, 1), const), pl.BlockSpec((nh2, 1), const)],
        out_specs=[pl.BlockSpec((2, cps, H_B, CHUNK, D_B), idx5),
                   pl.BlockSpec((2, cps, H_B, 2 * CHUNK, D_B), idx5),
                   pl.BlockSpec((2, cps, H_B, CHUNK, D_B), idx5),
                   pl.BlockSpec((2, cps, H_B, CHUNK, CHUNK), idx5),
                   pl.BlockSpec((cps, nh2, LANES), lambda b, c: (st(b, c), 0, 0))],
        out_shape=[jax.ShapeDtypeStruct((2, B * nc, H_B, CHUNK, D_B), F32),
                   jax.ShapeDtypeStruct((2, B * nc, H_B, 2 * CHUNK, D_B), BF16),
                   jax.ShapeDtypeStruct((2, B * nc, H_B, CHUNK, D_B), BF16),
                   jax.ShapeDtypeStruct((2, B * nc, H_B, CHUNK, CHUNK), BF16),
                   jax.ShapeDtypeStruct((B * nc, nh2, LANES), F32)],
        scratch_shapes=[pltpu.VMEM((rows + 2 * HALO, W3), F32)],
        compiler_params=_cparams("parallel", "parallel"), name="dn_prep",
    )(xb, xb, xb, ab, abT3, conv_w, al, dt, al.reshape(nh2, 1), dt.reshape(nh2, 1))


def _dn_scan_kernel(*refs):
    s0_ref = refs[0]
    per_dir = (refs[1:6], refs[6:11])
    of_ref, ob_ref, sfin_ref, s_ref = refs[11:15]
    c = pl.program_id(1)
    nc = pl.num_programs(1)

    @pl.when(c == 0)
    def _():
        s_ref[...] = s0_ref[0]

    chains = [(d, h) for d in range(2) for h in range(H_B)]
    states = [s_ref[d, h] for d, h in chains]
    rs = [_dot(per_dir[d][1][0, 0, h], s.astype(BF16)) for (d, h), s in zip(chains, states)]
    vns = [(per_dir[d][0][0, 0, h] - r[0:CHUNK]).astype(BF16) for (d, h), r in zip(chains, rs)]
    avs = [_dot(per_dir[d][3][0, 0, h], vn) for (d, h), vn in zip(chains, vns)]
    dss = [_dot_tn(per_dir[d][2][0, 0, h], vn) for (d, h), vn in zip(chains, vns)]
    for i, (d, h) in enumerate(chains):
        o_ref = of_ref if d == 0 else ob_ref
        o_ref[:, h * D_B:(h + 1) * D_B] = rs[i][CHUNK:] + avs[i]
        eg = per_dir[d][4][0, d * H_B + h:d * H_B + h + 1, :]
        s_ref[d, h] = states[i] * eg + dss[i]

    @pl.when(c == nc - 1)
    def _():
        sfin_ref[0] = s_ref[...]


def _dn_scan(prep, s0, B, T):
    u, wq, kg, a, eg = prep
    nc = T // CHUNK
    R = DN_ROWS
    nh2 = 2 * H_B

    def dir_specs(d):
        cb = (lambda b, c: b * nc + c) if d == 0 else (lambda b, c: b * nc + nc - 1 - c)
        idx5 = lambda b, c: (d, cb(b, c), 0, 0, 0)
        return [pl.BlockSpec((1, 1, H_B, CHUNK, D_B), idx5),
                pl.BlockSpec((1, 1, H_B, 2 * CHUNK, D_B), idx5),
                pl.BlockSpec((1, 1, H_B, CHUNK, D_B), idx5),
                pl.BlockSpec((1, 1, H_B, CHUNK, CHUNK), idx5),
                pl.BlockSpec((1, nh2, LANES), lambda b, c: (cb(b, c), 0, 0))]

    st = (1, 2, H_B, D_B, D_B)
    st_spec = pl.BlockSpec(st, lambda b, c: (b, 0, 0, 0, 0))
    return pl.pallas_call(
        _dn_scan_kernel, grid=(B, nc),
        in_specs=[st_spec] + dir_specs(0) + dir_specs(1),
        out_specs=[pl.BlockSpec((CHUNK, W_HEADS), lambda b, c: (b * nc + c, 0)),
                   pl.BlockSpec((CHUNK, W_HEADS), lambda b, c: (b * nc + nc - 1 - c, 0)),
                   st_spec],
        out_shape=[jax.ShapeDtypeStruct((B * T, W_HEADS), F32), jax.ShapeDtypeStruct((B * T, W_HEADS), F32),
                   jax.ShapeDtypeStruct((B,) + st[1:], F32)],
        scratch_shapes=[pltpu.VMEM(st[1:], F32)],
        compiler_params=_cparams("parallel", "arbitrary"), name="dn_scan",
    )(s0, u, wq, kg, a, eg, u, wq, kg, a, eg)


def _merge_kernel(x_ref, mod_ref, g1_ref, g2_ref, oa_ref, of_ref, ob_ref, zb_ref, gon_ref, oc_ref,
                  wg_ref, wa_ref, wb_ref, wc_ref, wo_ref, wrT_ref, x1_ref, h2_ref, lgT_ref):
    D = x_ref.shape[1]
    x = x_ref[...]
    sh1, sc1, gt1, sh2, sc2 = [mod_ref[0, j:j + 1, :] for j in range(5)]
    hb = (_rms(x, g1_ref[...]) * (1.0 + sc1) + sh1).astype(BF16)
    od = of_ref[...] + ob_ref[...]
    z = zb_ref[...]
    gon = gon_ref[...]
    obs = []
    for h in range(H_B):
        cs = slice(h * D_B, (h + 1) * D_B)
        obs.append(_rms(od[:, cs], gon) * _silu(z[:, cs]))
    ob = jnp.concatenate(obs, axis=1).astype(BF16)

    def gate(j):
        return _sigmoid(_dot(hb, wg_ref[:, j * D:(j + 1) * D]))

    merged = (gate(0) * _dot(oa_ref[...], wa_ref[...])
              + gate(1) * _dot(ob, wb_ref[...])
              + gate(2) * _dot(oc_ref[...], wc_ref[...]))
    x1 = x + gt1 * _dot(merged.astype(BF16), wo_ref[...])
    x1_ref[...] = x1
    h2 = _rms(x1, g2_ref[...]) * (1.0 + sc2) + sh2
    h2_ref[...] = h2
    lgT_ref[...] = lax.dot_general(wrT_ref[...], h2, (((1,), (1,)), ((), ())), preferred_element_type=F32,
                                   precision=lax.Precision.HIGHEST)


def _merge(x, mod5, g1, g2, oa, o_f, o_b, zb, g_onorm, oc, w_g, w_a, w_b, w_c, w_o, w_rT, T):
    n, D = x.shape
    tm = _pick_tile(T, 256)
    nt = T // tm
    W = W_HEADS
    E = w_rT.shape[0]
    row = lambda i: (i, 0)
    const = lambda i: (0, 0)
    full = lambda a: pl.BlockSpec(a.shape, const)
    return pl.pallas_call(
        _merge_kernel, grid=(n // tm,),
        in_specs=[pl.BlockSpec((tm, D), row), pl.BlockSpec((1, 8, D), lambda i: (i // nt, 0, 0)),
                  pl.BlockSpec((1, D), const), pl.BlockSpec((1, D), const),
                  pl.BlockSpec((tm, W), row), pl.BlockSpec((tm, W), row), pl.BlockSpec((tm, W), row),
                  pl.BlockSpec((tm, W), row), pl.BlockSpec((1, D_B), const), pl.BlockSpec((tm, W), row),
                  full(w_g), full(w_a), full(w_b), full(w_c), full(w_o), full(w_rT)],
        out_specs=[pl.BlockSpec((tm, D), row), pl.BlockSpec((tm, D), row), pl.BlockSpec((E, tm), lambda i: (0, i))],
        out_shape=[jax.ShapeDtypeStruct((n, D), F32), jax.ShapeDtypeStruct((n, D), F32),
                   jax.ShapeDtypeStruct((E, n), F32)],
        compiler_params=_cparams("parallel"), name="merge",
    )(x, mod5, g1, g2, oa, o_f, o_b, zb, g_onorm, oc, w_g, w_a, w_b, w_c, w_o, w_rT)


def _route_kernel(lg_ref, idx_ref, sel_ref, w_ref, *, cap):
    E, R, _ = lg_ref.shape
    lg = lg_ref[...]
    m = jnp.max(lg, axis=0, keepdims=True)
    ex = jnp.exp(lg - m)
    aff = ex / jnp.sum(ex, axis=0, keepdims=True)
    bits = pltpu.bitcast(aff, I32)

    def count(mask):
        t = jnp.sum(jnp.where(mask, 1.0, 0.0), axis=2, keepdims=True)
        return jnp.sum(t, axis=1, keepdims=True)

    def bisect(_, carry):
        lo, hi = carry
        mid = lo + lax.shift_right_logical(hi - lo, 1)
        ge = count(bits >= mid) >= cap
        return jnp.where(ge, mid, lo), jnp.where(ge, hi, mid)

    lo0 = jnp.zeros((E, 1, 1), I32)
    hi0 = jnp.full((E, 1, 1), 0x7F800000, I32)
    thr, _ = lax.fori_loop(0, 31, bisect, (lo0, hi0))
    gt = bits > thr
    eq = bits == thr
    need = cap - count(gt)

    lane_r = _iota((LANES, LANES), 0)
    lane_c = _iota((LANES, LANES), 1)
    u_lane = (lane_r <= lane_c).astype(BF16)
    row_r = _iota((R, R), 0)
    row_c = _iota((R, R), 1)
    sl_row = (row_c < row_r).astype(BF16)
    u_row = (row_r <= row_c).astype(BF16)
    ones8 = jnp.ones((8, LANES), BF16)
    eye_l = (lane_r == lane_c).astype(BF16)

    def prefix(mask_f32):
        w = _dot(mask_f32.astype(BF16), u_lane)
        tot = (jnp.zeros((R, LANES), F32) + w[:, LANES - 1:LANES]).astype(BF16)
        return w + _dot(sl_row, tot), w

    gtf = jnp.where(gt, 1.0, 0.0)
    eqf = jnp.where(eq, 1.0, 0.0)
    for e in range(E):
        gt_e = gtf[e]
        eq_e = eqf[e]
        pe, _ = prefix(eq_e)
        sel_e = jnp.maximum(gt_e, jnp.where(pe <= need[e], eq_e, 0.0))
        sel_ref[e] = sel_e
        _, w_e = prefix(sel_e)
        w_ref[e] = w_e

    nblk = cap // LANES
    s_base = _iota((LANES, 1), 0).astype(F32)

    def slot_block(i, _):
        e = i // nblk
        jb = i % nblk
        sel_e = sel_ref[e]
        w_e = w_ref[e]
        s_col = s_base + jnp.asarray(jb * LANES, F32)
        rt_row = _dot_nt(ones8, sel_e.astype(BF16))
        rc_row = _dot(rt_row.astype(BF16), u_row)[0:1, :]
        below = rc_row <= s_col
        rho = jnp.sum(jnp.where(below, 1.0, 0.0), axis=1, keepdims=True)
        before = jnp.max(jnp.where(below, rc_row, 0.0), axis=1, keepdims=True)
        onehot = (_iota((LANES, R), 1).astype(F32) == rho).astype(BF16)
        wg = _dot(onehot, w_e.astype(BF16))
        lam = jnp.sum(jnp.where(wg <= s_col - before, 1.0, 0.0), axis=1, keepdims=True)
        rho_l = _dot_nt(ones8, (eye_l * rho).astype(BF16))
        lam_l = _dot_nt(ones8, (eye_l * lam).astype(BF16))
        idx_ref[i] = (rho_l[0:1, :] * LANES + lam_l[0:1, :]).astype(I32)
        return 0

    lax.fori_loop(0, E * nblk, slot_block, 0)


def _route(logits_t, cap):
    E, n = logits_t.shape
    R = n // LANES
    nblk = cap // LANES
    idx = pl.pallas_call(
        functools.partial(_route_kernel, cap=cap),
        out_shape=jax.ShapeDtypeStruct((E * nblk, 1, LANES), I32),
        scratch_shapes=[pltpu.VMEM((E, R, LANES), F32), pltpu.VMEM((E, R, LANES), F32)],
        compiler_params=pltpu.CompilerParams(vmem_limit_bytes=VMEM_LIMIT_BYTES), name="route",
    )(logits_t.reshape(E, R, LANES))
    return idx.reshape(E, cap)


def _moe_kernel(idx_ref, idxn_ref, h_hbm, wr_ref, wg_ref, wu_ref, wd_ref, acc_in, acc_hbm,
                xbuf, mbuf, sem_x, sem_m, sem_s, *, nt):
    del acc_in
    e = pl.program_id(0)
    k = pl.program_id(1)
    total = pl.num_programs(0) * nt
    step = e * nt + k
    slot = step % 2
    tm = xbuf.shape[1]

    def row_copies(src_of, dst_of, sem, ids):
        def start(r, _):
            t = ids[0, 0, r]
            pltpu.make_async_copy(src_of(t, r), dst_of(t, r), sem).start()
            return 0
        lax.fori_loop(0, tm, start, 0, unroll=8)

    def wait_rows(sem, buf):
        pltpu.make_async_copy(buf, buf, sem).wait()

    def gather_x(ids, s):
        row_copies(lambda t, r: h_hbm.at[pl.ds(t, 1)], lambda t, r: xbuf.at[s, pl.ds(r, 1)], sem_x.at[s], ids)

    @pl.when(step == 0)
    def _():
        gather_x(idx_ref, 0)

    wait_rows(sem_x.at[slot], xbuf.at[slot])

    @pl.when(step + 1 < total)
    def _():
        gather_x(idxn_ref, 1 - slot)

    @pl.when(step > 0)
    def _():
        wait_rows(sem_s, mbuf)
    row_copies(lambda t, r: acc_hbm.at[pl.ds(t, 1)], lambda t, r: mbuf.at[pl.ds(r, 1)], sem_m, idx_ref)

    xe = xbuf[slot].astype(BF16)
    hid = _silu(_dot(xe, wg_ref[0])) * _dot(xe, wu_ref[0])
    lg = _dot(xe, wr_ref[...])
    lane = _iota(lg.shape, 1)
    lg = jnp.where(lane < N_EXPERTS, lg, NEG)
    ex = jnp.exp(lg - jnp.max(lg, axis=-1, keepdims=True))
    aff = ex / jnp.sum(ex, axis=-1, keepdims=True)
    gate = jnp.sum(jnp.where(lane == e, aff, 0.0), axis=-1, keepdims=True)
    ye = _dot(hid.astype(BF16), wd_ref[0]) * gate

    wait_rows(sem_m, mbuf)
    mbuf[...] = mbuf[...] + ye
    row_copies(lambda t, r: mbuf.at[pl.ds(r, 1)], lambda t, r: acc_hbm.at[pl.ds(t, 1)], sem_s, idx_ref)

    @pl.when(step == total - 1)
    def _():
        wait_rows(sem_s, mbuf)


def _moe(h2, idx, w_router_pad, w_gate, w_up, w_down, acc):
    n, D = h2.shape
    E, cap = idx.shape
    F = w_gate.shape[-1]
    tm = _pick_tile(cap, 256)
    nt = cap // tm
    idx2 = idx.reshape(E * nt, 1, tm)
    last = E * nt - 1
    smem_blk = lambda f: pl.BlockSpec((1, 1, tm), f, memory_space=pltpu.SMEM)
    return pl.pallas_call(
        functools.partial(_moe_kernel, nt=nt), grid=(E, nt),
        in_specs=[smem_blk(lambda e, k: (e * nt + k, 0, 0)),
                  smem_blk(lambda e, k: (jnp.minimum(e * nt + k + 1, last), 0, 0)),
                  pl.BlockSpec(memory_space=pl.ANY),
                  pl.BlockSpec(w_router_pad.shape, lambda e, k: (0, 0)),
                  pl.BlockSpec((1, D, F), lambda e, k: (e, 0, 0)),
                  pl.BlockSpec((1, D, F), lambda e, k: (e, 0, 0)),
                  pl.BlockSpec((1, F, D), lambda e, k: (e, 0, 0)),
                  pl.BlockSpec(memory_space=pl.ANY)],
        out_specs=pl.BlockSpec(memory_space=pl.ANY),
        out_shape=jax.ShapeDtypeStruct((n, D), F32),
        input_output_aliases={7: 0},
        scratch_shapes=[pltpu.VMEM((2, tm, D), F32), pltpu.VMEM((tm, D), F32),
                        pltpu.SemaphoreType.DMA((2,)), pltpu.SemaphoreType.DMA(()), pltpu.SemaphoreType.DMA(())],
        compiler_params=_cparams("arbitrary", "arbitrary"), name="moe",
    )(idx2, idx2, h2, w_router_pad, w_gate, w_up, w_down, acc)


def _resid_kernel(x_ref, m_ref, gt_ref, g_ref, o_ref, *, final):
    x2 = x_ref[...] + gt_ref[0] * m_ref[...]
    o_ref[...] = _rms(x2, g_ref[...]) if final else x2


def _resid(x1, m, gt2, g_final, T, final):
    n, D = x1.shape
    tm = _pick_tile(T, 512)
    nt = T // tm
    row = lambda i: (i, 0)
    return pl.pallas_call(
        functools.partial(_resid_kernel, final=final), grid=(n // tm,),
        in_specs=[pl.BlockSpec((tm, D), row), pl.BlockSpec((tm, D), row),
                  pl.BlockSpec((1, 1, D), lambda i: (i // nt, 0, 0)), pl.BlockSpec((1, D), lambda i: (0, 0))],
        out_specs=pl.BlockSpec((tm, D), row), out_shape=jax.ShapeDtypeStruct((n, D), F32),
        compiler_params=_cparams("parallel"), name="resid",
    )(x1, m, gt2, g_final)


def _lambda_init(layer):
    return 0.8 - 0.6 * math.exp(-0.3 * layer)


def _prep_weights(p, l):
    W = W_HEADS
    D = p['w_in'].shape[1]
    w_in = p['w_in'][l]
    nab = 4 * H_B
    o_ab = 3 * W + 4 * W
    o_c = o_ab + nab
    o_g = o_c + 3 * W
    w_main = jnp.concatenate([w_in[:, :o_ab], w_in[:, o_c:o_g]], axis=1).astype(BF16)
    w_ab = w_in[:, o_ab:o_c]
    E = p['w_router'].shape[-1]
    w_r = p['w_router'][l]
    return dict(
        w_main=w_main, w_ab=w_ab, w_abT=w_ab.T,
        w_g=w_in[:, o_g:].astype(BF16),
        w_a=p['w_br_a'][l].astype(BF16), w_b=p['w_br_b'][l].astype(BF16), w_c=p['w_br_c'][l].astype(BF16),
        w_o=p['w_out'][l].astype(BF16),
        w_rT=w_r.T,
        w_r_pad=jnp.pad(w_r, ((0, 0), (0, LANES - E))).astype(BF16),
        w_gate=p['w_e_gate'][l].astype(BF16), w_up=p['w_e_up'][l].astype(BF16), w_down=p['w_e_down'][l].astype(BF16),
        g1=p['g_norm1'][l][None], g2=p['g_norm2'][l][None],
        conv_w=p['conv_w'][l], a_log=p['a_log'][l], dt_bias=p['dt_bias'][l],
        g_onorm=p['g_onorm'][l][None], lam_qk=p['lam_qk'][l], g_subln=p['g_subln'][l][None], rpb=p['rpb'][l],
    )


def _trunk_layer(x, mod, lw, l, B, T, g_final, final, ctx=None):
    n, D = x.shape
    lam_init = _lambda_init(l)
    latent = ctx is not None
    sh1, sc1 = mod[:, 0:1], mod[:, 1:2]
    outs = _inproj(x, sc1, sh1, lw['g1'], lw['w_main'], lw['w_ab'], lw['w_abT'], T,
                   _rope_tables(T) if latent else None, emit_kv=not latent)
    qa, ka, va, xb, zb, qc, kc, vc, ab, abT = outs[:10]
    prep = _dn_prep(xb, ab, abT, lw['conv_w'], lw['a_log'], lw['dt_bias'], B, T)
    if latent:
        ck_a, cv_a, ck_c, cv_c, c_state = ctx
        oa = _diff_attn(qa, ka, va, ck_a, cv_a, lw['lam_qk'], lw['g_subln'], B, T, lam_init)
        o_f, o_b, _ = _dn_scan(prep, c_state, B, T)
        oc = _na_attn(qc, kc, vc, ck_c, cv_c, lw['rpb'], B, T)
        new_ctx = None
    else:
        oa = _diff_attn(qa, ka, va, None, None, lw['lam_qk'], lw['g_subln'], B, T, lam_init)
        s0 = jnp.zeros((B, 2, H_B, D_B, D_B), F32)
        o_f, o_b, s_fin = _dn_scan(prep, s0, B, T)
        oc = _dense_attn(qc, kc, vc, B, T)
        new_ctx = tuple(a.reshape(B, T, 4, 128) for a in outs[10:14]) + (s_fin,)
    mod5 = jnp.pad(mod[:, :5], ((0, 0), (0, 3), (0, 0)))
    x1, h2, lg_t = _merge(x, mod5, lw['g1'], lw['g2'], oa, o_f, o_b, zb, lw['g_onorm'], oc,
                          lw['w_g'], lw['w_a'], lw['w_b'], lw['w_c'], lw['w_o'], lw['w_rT'], T)
    cap = EC_FACTOR * n // N_EXPERTS
    idx = _route(lg_t, cap)
    m = _moe(h2, idx, lw['w_r_pad'], lw['w_gate'], lw['w_up'], lw['w_down'], jnp.zeros((n, D), F32))
    x2 = _resid(x1, m, mod[:, 5:6], g_final, T, final)
    return x2, new_ctx


def kernel(x_prompt, x_sample, c, cache_diff_k, cache_diff_v, cache_na_k, cache_na_v, state_delta, c_ctx, w_ada, b_ada, g_norm1, g_norm2, w_in, conv_w, a_log, dt_bias, g_onorm, lam_qk, g_subln, rpb, w_br_a, w_br_b, w_br_c, w_out, w_router, w_e_gate, w_e_up, w_e_down, g_final):
    p = dict(w_in=w_in, conv_w=conv_w, a_log=a_log, dt_bias=dt_bias, g_onorm=g_onorm, lam_qk=lam_qk,
             g_subln=g_subln, rpb=rpb, w_br_a=w_br_a, w_br_b=w_br_b, w_br_c=w_br_c, w_out=w_out,
             w_router=w_router, w_e_gate=w_e_gate, w_e_up=w_e_up, w_e_down=w_e_down,
             g_norm1=g_norm1, g_norm2=g_norm2)
    depth = w_in.shape[0]
    Bp, Tp, D = x_prompt.shape
    Bs, Ts, _ = x_sample.shape
    gf = g_final[None]

    nrow = 1 + Bs
    cond = jnp.pad(jnp.concatenate([c_ctx[None], c], axis=0), ((0, -nrow % 8), (0, 0)))
    mod = _adaln(cond, w_ada, b_ada).reshape(depth, -1, 6, D)
    layers = [_prep_weights(p, l) for l in range(depth)]

    xp = x_prompt.reshape(Bp * Tp, D)
    new_ctx = []
    for l in range(depth):
        mod_ctx = jnp.broadcast_to(mod[l, 0:1], (Bp, 6, D))
        xp, nc = _trunk_layer(xp, mod_ctx, layers[l], l, Bp, Tp, gf, l == depth - 1)
        new_ctx.append(nc)

    xs = x_sample.reshape(Bs * Ts, D)
    for l in range(depth):
        Lc = cache_diff_k.shape[2]
        flat = lambda a: a[:, l].reshape(Bs * Lc, W_HEADS).astype(BF16)
        ctx = (flat(cache_diff_k), flat(cache_diff_v), flat(cache_na_k), flat(cache_na_v), state_delta[:, l])
        xs, _ = _trunk_layer(xs, mod[l, 1:1 + Bs], layers[l], l, Bs, Ts, gf, l == depth - 1, ctx)

    stack = lambda j: jnp.stack([nc[j] for nc in new_ctx], axis=1)
    return (xp.reshape(Bp, Tp, D), xs.reshape(Bs, Ts, D), stack(0), stack(1), stack(2), stack(3), stack(4))
```

```python
import functools
import math

import jax
import jax.numpy as jnp
from jax import lax
from jax.experimental import pallas as pl
from jax.experimental.pallas import tpu as pltpu

F32 = jnp.float32
BF16 = jnp.bfloat16
I32 = jnp.int32

H_A, D_A = 4, 64
H_B, D_B = 4, 128
H_C, D_C = 4, 128
CONV_K = 5
CHUNK = 64
GRID_W = 64
WIN_R, WIN_C = 8, 16
N_EXPERTS = 16
EC_FACTOR = 2
ROPE_BASE = 10000.0
EPS = 1e-6
NEG = -1e30
W_HEADS = 512
ATTN_KEY_CHUNK = 1024
LOG2E = 1.4426950408889634
ROUTE_BLOCKS_PER_STEP = 4

LANES = 128
VMEM_LIMIT_BYTES = 56 * 1024 * 1024


def _cparams(*sem):
    return pltpu.CompilerParams(dimension_semantics=sem, vmem_limit_bytes=VMEM_LIMIT_BYTES)


def _dot(a, b):
    return jnp.dot(a, b, preferred_element_type=F32)


def _dot_nt(a, b):
    return lax.dot_general(a, b, (((1,), (1,)), ((), ())), preferred_element_type=F32)


def _dot_tn(a, b):
    return lax.dot_general(a, b, (((0,), (0,)), ((), ())), preferred_element_type=F32)


def _dot_f32(a, b):
    return jnp.dot(a, b, preferred_element_type=F32, precision=lax.Precision.HIGHEST)


def _sigmoid(x):
    return 1.0 / (1.0 + jnp.exp(-x))


def _silu(x):
    return x * _sigmoid(x)


def _softplus(x):
    return jnp.maximum(x, 0.0) + jnp.log(1.0 + jnp.exp(-jnp.abs(x)))


def _rms(x, g):
    return x * lax.rsqrt(jnp.mean(x * x, axis=-1, keepdims=True) + EPS) * g


def _iota(shape, dim):
    return lax.broadcasted_iota(I32, shape, dim)


def _pick_tile(n, pref):
    t = min(n, pref)
    while n % t:
        t //= 2
    return t


def _adaln_kernel(c_ref, w_ref, b_ref, o_ref):
    c = c_ref[...]
    s = _silu(c).astype(BF16)
    o_ref[0] = _dot(s, w_ref[0].astype(BF16)) + b_ref[0]


def _adaln(cond, w_ada, b_ada):
    L, D, N = w_ada.shape
    R = cond.shape[0]
    tn = _pick_tile(N, 1536)
    return pl.pallas_call(
        _adaln_kernel,
        grid=(L, N // tn),
        in_specs=[pl.BlockSpec((R, D), lambda l, j: (0, 0)),
                  pl.BlockSpec((1, D, tn), lambda l, j: (l, 0, j)),
                  pl.BlockSpec((1, 1, tn), lambda l, j: (l, 0, j))],
        out_specs=pl.BlockSpec((1, R, tn), lambda l, j: (l, 0, j)),
        out_shape=jax.ShapeDtypeStruct((L, R, N), F32),
        compiler_params=_cparams("parallel", "parallel"),
        name="adaln",
    )(cond, w_ada, b_ada.reshape(L, 1, N))


def _inproj_kernel(*refs, rope, emit_kv):
    it = iter(refs)
    x_ref, sc_ref, sh_ref, g_ref, w_ref, wab_ref = [next(it) for _ in range(6)]
    if rope:
        cos_ref, sa_ref, sb_ref = [next(it) for _ in range(3)]
    qa_ref, ka_ref, va_ref, xb_ref, zb_ref, qc_ref, kc_ref, vc_ref, ab_ref = [next(it) for _ in range(9)]
    if emit_kv:
        ka32_ref, va32_ref, kc32_ref, vc32_ref = [next(it) for _ in range(4)]

    x = x_ref[...]
    h = _rms(x, g_ref[...]) * (1.0 + sc_ref[0]) + sh_ref[0]
    hb = h.astype(BF16)
    W = W_HEADS

    def proj(j, n=1):
        return _dot(hb, w_ref[:, j * W:(j + n) * W])

    def roped(v):
        if not rope:
            return v
        n = v.shape[-1]
        return v * cos_ref[...] + pltpu.roll(v, n - D_A // 4, 1) * sa_ref[...] + pltpu.roll(v, D_A // 4, 1) * sb_ref[...]

    qa = proj(0)
    qa_ref[...] = (roped(qa) * (D_A ** -0.5 * LOG2E)).astype(BF16)
    ka = proj(1)
    ka_ref[...] = roped(ka).astype(BF16)
    va = proj(2)
    va_ref[...] = va.astype(BF16)
    xb_ref[...] = proj(3, 3)
    zb_ref[...] = proj(6)
    qc_ref[...] = proj(7).astype(BF16)
    kc = proj(8)
    kc_ref[...] = kc.astype(BF16)
    vc = proj(9)
    vc_ref[...] = vc.astype(BF16)
    ab_ref[...] = _dot(hb, wab_ref[...])
    if emit_kv:
        ka32_ref[...] = ka
        va32_ref[...] = va
        kc32_ref[...] = kc
        vc32_ref[...] = vc


def _inproj(x, sc, sh, g, w_main, w_ab, T, rope_tabs, emit_kv):
    n, D = x.shape
    tm = _pick_tile(T, 512)
    nt = T // tm
    W = W_HEADS
    rope = rope_tabs is not None
    row = lambda i: (i, 0)
    per_b = lambda i: (i // nt, 0, 0)
    const = lambda i: (0, 0)
    in_specs = [pl.BlockSpec((tm, D), row),
                pl.BlockSpec((1, 1, D), per_b), pl.BlockSpec((1, 1, D), per_b),
                pl.BlockSpec((1, D), const),
                pl.BlockSpec(w_main.shape, const), pl.BlockSpec(w_ab.shape, const)]
    args = [x, sc, sh, g, w_main, w_ab]
    if rope:
        in_specs += [pl.BlockSpec((tm, W), lambda i: (i % nt, 0))] * 3
        args += list(rope_tabs)
    bf = lambda: jax.ShapeDtypeStruct((n, W), BF16)
    f32 = lambda w: jax.ShapeDtypeStruct((n, w), F32)
    out_shape = [bf(), bf(), bf(), f32(3 * W), f32(W), bf(), bf(), bf(), f32(16)]
    out_specs = [pl.BlockSpec((tm, W), row)] * 3 + [pl.BlockSpec((tm, 3 * W), row), pl.BlockSpec((tm, W), row)] \
        + [pl.BlockSpec((tm, W), row)] * 3 + [pl.BlockSpec((tm, 16), row)]
    if emit_kv:
        out_shape += [f32(W)] * 4
        out_specs += [pl.BlockSpec((tm, W), row)] * 4
    return pl.pallas_call(
        functools.partial(_inproj_kernel, rope=rope, emit_kv=emit_kv),
        grid=(n // tm,), in_specs=in_specs, out_specs=out_specs, out_shape=out_shape,
        compiler_params=_cparams("parallel"), name="inproj",
    )(*args)


def _rope_tables(T):
    quarter = D_A // 4
    inv = 1.0 / (ROPE_BASE ** (jnp.arange(quarter, dtype=F32) / quarter))
    t = jnp.arange(T)
    pos = jnp.stack([t // GRID_W, t % GRID_W], axis=-1).astype(F32)
    ang = pos[:, :, None] * inv
    cos, sin = jnp.cos(ang), jnp.sin(ang)
    cos64 = jnp.concatenate([cos[:, 0], cos[:, 0], cos[:, 1], cos[:, 1]], axis=-1)
    zero = jnp.zeros_like(sin[:, 0])
    sa64 = jnp.concatenate([-sin[:, 0], zero, -sin[:, 1], zero], axis=-1)
    sb64 = jnp.concatenate([zero, sin[:, 0], zero, sin[:, 1]], axis=-1)
    rep = W_HEADS // D_A
    return tuple(jnp.tile(a, (1, rep)) for a in (cos64, sa64, sb64))


def _diff_attn_kernel(*refs, has_cache, lam_init):
    if has_cache:
        q_ref, k_ref, v_ref, kc_ref, vc_ref, lq_ref, g_ref, o_ref = refs
    else:
        q_ref, k_ref, v_ref, lq_ref, g_ref, o_ref = refs
    q = q_ref[...]
    lane = _iota(q.shape, 1)
    zero = jnp.zeros_like(q)
    qs = (jnp.where(lane < D_A, q, zero), jnp.where(lane >= D_A, q, zero))
    lq = lq_ref[...]
    lam = (jnp.exp(jnp.sum(lq[0:1] * lq[1:2], axis=-1, keepdims=True))
           - jnp.exp(jnp.sum(lq[2:3] * lq[3:4], axis=-1, keepdims=True)) + lam_init)
    hd = q.shape[1]

    def with_ones(v):
        one = jnp.where(_iota(v.shape, 1) == 0, 1.0, 0.0).astype(BF16)
        return jnp.concatenate([v, one], axis=1)

    T = k_ref.shape[0]
    tk = _pick_tile(T, ATTN_KEY_CHUNK)
    chunks = [(k_ref[j * tk:(j + 1) * tk, :], with_ones(v_ref[j * tk:(j + 1) * tk, :])) for j in range(T // tk)]
    if has_cache:
        chunks.append((kc_ref[...], with_ones(vc_ref[...])))
    tq = q.shape[0]
    outs = []
    for comp in range(2):
        m = jnp.full((tq, 1), NEG, F32)
        acc = jnp.zeros((tq, 2 * hd), F32)
        for kj, vxj in chunks:
            s = _dot_nt(qs[comp], kj)
            m_new = jnp.maximum(m, jnp.max(s, axis=-1, keepdims=True))
            acc = acc * jnp.exp2(m - m_new) + _dot(jnp.exp2(s - m_new).astype(BF16), vxj)
            m = m_new
        outs.append(acc[:, :hd] * (1.0 / acc[:, hd:hd + 1]))
    o = outs[0] - lam * outs[1]
    o_ref[...] = (_rms(o, g_ref[...]) * (1.0 - lam_init)).astype(BF16)


def _diff_attn(q, k, v, cache_k, cache_v, lam_qk, g_sub, B, T, lam_init):
    n = q.shape[0]
    tq = _pick_tile(T, 256)
    nq = T // tq
    has_cache = cache_k is not None
    hd = 2 * D_A
    in_specs = [pl.BlockSpec((tq, hd), lambda b, h, i: (b * nq + i, h)),
                pl.BlockSpec((T, hd), lambda b, h, i: (b, h)),
                pl.BlockSpec((T, hd), lambda b, h, i: (b, h))]
    args = [q, k, v]
    if has_cache:
        Lc = cache_k.shape[0] // B
        in_specs += [pl.BlockSpec((Lc, hd), lambda b, h, i: (b, h))] * 2
        args += [cache_k, cache_v]
    in_specs += [pl.BlockSpec(lam_qk.shape, lambda b, h, i: (0, 0)), pl.BlockSpec((1, hd), lambda b, h, i: (0, 0))]
    args += [lam_qk, g_sub]
    return pl.pallas_call(
        functools.partial(_diff_attn_kernel, has_cache=has_cache, lam_init=lam_init),
        grid=(B, H_A, nq), in_specs=in_specs,
        out_specs=pl.BlockSpec((tq, hd), lambda b, h, i: (b * nq + i, h)),
        out_shape=jax.ShapeDtypeStruct((n, W_HEADS), BF16),
        compiler_params=_cparams("parallel", "parallel", "parallel"), name="diff_attn",
    )(*args)


def _softmax_pv(heads):
    def fold(xs, op):
        out = xs[0]
        for x in xs[1:]:
            out = op(out, x)
        return out

    ms = [fold([jnp.max(s, axis=-1, keepdims=True) for s, _ in parts], jnp.maximum) for parts in heads]
    ps = [[jnp.exp(s - m) for s, _ in parts] for parts, m in zip(heads, ms)]
    rs = [1.0 / fold([jnp.sum(p, axis=-1, keepdims=True) for p in pp], jnp.add) for pp in ps]
    pbs = [[(p * r).astype(BF16) for p in pp] for pp, r in zip(ps, rs)]
    return [fold([_dot(p, v) for p, (_, v) in zip(pp, parts)], jnp.add) for pp, parts in zip(pbs, heads)]


def _na_kernel(q_ref, k_ref, v_ref, kc_ref, vc_ref, bias_ref, o_ref, *, rows, wr):
    r = pl.program_id(1)
    rs = jnp.clip(r - wr // 2, 0, rows - wr)
    start = pl.multiple_of(rs * GRID_W, GRID_W)
    nloc = wr * GRID_W
    scale = D_C ** -0.5
    cols = [slice(h * D_C, (h + 1) * D_C) for h in range(H_C)]
    qs = [q_ref[:, cs] for cs in cols]
    s_loc = [_dot_nt(q, k_ref[pl.ds(start, nloc), cs]) for q, cs in zip(qs, cols)]
    s_ctx = [_dot_nt(q, kc_ref[:, cs]) for q, cs in zip(qs, cols)]
    heads = [[(s_loc[h] * scale + bias_ref[0, h], v_ref[pl.ds(start, nloc), cols[h]]),
              (s_ctx[h] * scale, vc_ref[:, cols[h]])] for h in range(H_C)]
    for cs, o in zip(cols, _softmax_pv(heads)):
        o_ref[:, cs] = o.astype(BF16)


def _na_bias_table(rpb, rows, wr):
    col = jnp.arange(GRID_W)
    cs = jnp.clip(col - WIN_C // 2, 0, GRID_W - WIN_C)
    col_mask = (col[None, :] >= cs[:, None]) & (col[None, :] < cs[:, None] + WIN_C)
    dc_idx = jnp.clip(col[None, :] - col[:, None] + WIN_C - 1, 0, 2 * WIN_C - 2)
    onehot = (dc_idx[:, :, None] == jnp.arange(2 * WIN_C - 1)).astype(F32)
    base = jnp.einsum('hdj,qkj->hdqk', rpb.astype(F32), onehot, precision=lax.Precision.HIGHEST)
    base = jnp.where(col_mask[None, None], base, NEG)
    tabs = []
    for off in range(wr):
        lo = WIN_R - 1 - off
        tabs.append(base[:, lo:lo + wr].transpose(0, 2, 1, 3).reshape(H_C, GRID_W, wr * GRID_W))
    return jnp.stack(tabs)


def _na_attn(q, k, v, cache_k, cache_v, rpb, B, T):
    n = q.shape[0]
    rows = T // GRID_W
    wr = min(WIN_R, rows)
    Lc = cache_k.shape[0] // B
    bias = _na_bias_table(rpb, rows, wr)
    W = W_HEADS

    def bias_idx(b, r):
        return (r - jnp.clip(r - wr // 2, 0, rows - wr), 0, 0, 0)

    return pl.pallas_call(
        functools.partial(_na_kernel, rows=rows, wr=wr),
        grid=(B, rows),
        in_specs=[pl.BlockSpec((GRID_W, W), lambda b, r: (b * rows + r, 0)),
                  pl.BlockSpec((T, W), lambda b, r: (b, 0)),
                  pl.BlockSpec((T, W), lambda b, r: (b, 0)),
                  pl.BlockSpec((Lc, W), lambda b, r: (b, 0)),
                  pl.BlockSpec((Lc, W), lambda b, r: (b, 0)),
                  pl.BlockSpec((1, H_C, GRID_W, wr * GRID_W), bias_idx)],
        out_specs=pl.BlockSpec((GRID_W, W), lambda b, r: (b * rows + r, 0)),
        out_shape=jax.ShapeDtypeStruct((n, W), BF16),
        compiler_params=_cparams("parallel", "arbitrary"), name="na_attn",
    )(q, k, v, cache_k, cache_v, bias)


def _dense_attn_kernel(q_ref, k_ref, v_ref, o_ref):
    scale = D_C ** -0.5
    cols = [slice(h * D_C, (h + 1) * D_C) for h in range(H_C)]
    ss = [_dot_nt(q_ref[:, cs], k_ref[:, cs]) for cs in cols]
    heads = [[(s * scale, v_ref[:, cs])] for s, cs in zip(ss, cols)]
    for cs, o in zip(cols, _softmax_pv(heads)):
        o_ref[:, cs] = o.astype(BF16)


def _dense_attn(q, k, v, B, T):
    n = q.shape[0]
    tq = _pick_tile(T, 256)
    nq = T // tq
    W = W_HEADS
    return pl.pallas_call(
        _dense_attn_kernel, grid=(B, nq),
        in_specs=[pl.BlockSpec((tq, W), lambda b, i: (b * nq + i, 0)),
                  pl.BlockSpec((T, W), lambda b, i: (b, 0)),
                  pl.BlockSpec((T, W), lambda b, i: (b, 0))],
        out_specs=pl.BlockSpec((tq, W), lambda b, i: (b * nq + i, 0)),
        out_shape=jax.ShapeDtypeStruct((n, W), BF16),
        compiler_params=_cparams("parallel", "parallel"), name="dense_attn",
    )(q, k, v)


HALO = 8
DN_ROWS = H_B * CHUNK
INV_BASE = 8
DN_CHUNKS_PER_STEP = 2


def _bmm(a, b):
    return lax.dot_general(a, b, (((2,), (1,)), ((0,), (0,))), preferred_element_type=F32)


def _bmm_nt(a, b):
    return lax.dot_general(a, b, (((2,), (2,)), ((0,), (0,))), preferred_element_type=F32)


def _bmmb(a, b):
    return _bmm(a.astype(BF16), b.astype(BF16))


def _dn_prep_kernel(x_ref, xp_ref, xn_ref, ab_ref, abT_ref, cw_ref, alr_ref, dtr_ref, alc_ref, dtc_ref,
                    u_ref, wq_ref, kg_ref, a_ref, eg_ref, xs_ref):
    c = pl.program_id(1)
    nsteps = pl.num_programs(1)
    cps = abT_ref.shape[0]
    rows = cps * CHUNK
    pad = CONV_K // 2
    xs_ref[HALO:HALO + rows, :] = x_ref[...]
    xs_ref[0:HALO, :] = jnp.where(c > 0, xp_ref[...], 0.0)
    xs_ref[HALO + rows:, :] = jnp.where(c < nsteps - 1, xn_ref[...], 0.0)
    y = None
    for j in range(CONV_K):
        t = cw_ref[j:j + 1, :] * xs_ref[HALO - pad + j:HALO - pad + j + rows, :]
        y = t if y is None else y + t
    y = _silu(y)
    parts = []
    for j in range(cps):
        parts += _dn_chunk_terms(y[j * CHUNK:(j + 1) * CHUNK], ab_ref[j * CHUNK:(j + 1) * CHUNK, :], abT_ref[j],
                                 alr_ref, dtr_ref, alc_ref, dtc_ref, eg_ref, j)
    cat = lambda key: jnp.concatenate([p[key] for p in parts], axis=0)
    L = cat('L')
    ri = _iota((CHUNK, CHUNK), 0)
    ci = _iota((CHUNK, CHUNK), 1)
    inb = (ri // INV_BASE) == (ci // INV_BASE)
    Dg = jnp.where(inb, L, 0.0)
    X = (ri == ci).astype(F32) - Dg
    P = _bmmb(Dg, Dg)
    X = X + _bmmb(X, P)
    X = X + _bmmb(X, _bmmb(P, P))
    bs = INV_BASE
    while bs < CHUNK:
        outer = (ri // (2 * bs)) == (ci // (2 * bs))
        C = jnp.where(outer & jnp.logical_not(inb), L, 0.0)
        Xb = X.astype(BF16)
        X = X - _bmm(_bmm(Xb, C.astype(BF16)).astype(BF16), Xb)
        inb = outer
        bs *= 2
    Xb = X.astype(BF16)
    u = _bmm(Xb, cat('vb').astype(BF16))
    w = _bmm(Xb, cat('kbg').astype(BF16))
    for i, p in enumerate(parts):
        d, j = p['d'], p['j']
        rows_i = slice(i * H_B, (i + 1) * H_B)
        u_ref[d, j] = u[rows_i]
        wq_ref[d, j] = jnp.concatenate([w[rows_i], p['qg']], axis=1).astype(BF16)
        kg_ref[d, j] = p['kg'].astype(BF16)
        a_ref[d, j] = p['a_in'].astype(BF16)


def _dn_chunk_terms(y, ab, abT, alr_ref, dtr_ref, alc_ref, dtc_ref, eg_ref, j):
    W = W_HEADS

    def l2n(v):
        return v * lax.rsqrt(jnp.sum(v * v, axis=-1, keepdims=True) + EPS)

    q = jnp.stack([l2n(y[:, h * D_B:(h + 1) * D_B]) * (D_B ** -0.5) for h in range(H_B)])
    k = jnp.stack([l2n(y[:, W + h * D_B:W + (h + 1) * D_B]) for h in range(H_B)])
    v = jnp.stack([y[:, 2 * W + h * D_B:2 * W + (h + 1) * D_B] for h in range(H_B)])
    kb16 = k.astype(BF16)
    kk = _bmm_nt(kb16, kb16)
    qk = _bmm_nt(q.astype(BF16), kb16)

    nh2 = 2 * H_B
    g_col = -jnp.exp(alr_ref[...]) * _softplus(ab[:, 0:nh2] + dtr_ref[...])
    beta_col = _sigmoid(ab[:, nh2:2 * nh2])
    g_row = -jnp.exp(alc_ref[...]) * _softplus(abT[0:nh2, :] + dtc_ref[...])
    ri = _iota((CHUNK, CHUNK), 0)
    ci = _iota((CHUNK, CHUNK), 1)
    lo = (ri >= ci).astype(F32)
    up = (ri <= ci).astype(F32)
    gc = jnp.where(_iota((CHUNK, nh2), 1) < H_B, _dot_f32(lo, g_col), _dot_f32(up, g_col))
    gr = jnp.where(_iota((nh2, CHUNK), 0) < H_B, _dot_f32(g_row, up), _dot_f32(g_row, lo))
    glast = jnp.sum(g_col, axis=0, keepdims=True)
    eg_ref[j] = jnp.exp(jnp.sum(g_row, axis=1, keepdims=True)) * jnp.ones((nh2, LANES), F32)

    terms = []
    for d in range(2):
        cols = [d * H_B + h for h in range(H_B)]
        gc_h = jnp.stack([gc[:, i:i + 1] for i in cols])
        beta_h = jnp.stack([beta_col[:, i:i + 1] for i in cols])
        gl_h = jnp.stack([glast[:, i:i + 1] for i in cols])
        gr_h = jnp.stack([gr[i:i + 1, :] for i in cols])
        tri = (ri >= ci) if d == 0 else (ri <= ci)
        strict = (ri > ci) if d == 0 else (ri < ci)
        decay = jnp.where(tri, jnp.exp(jnp.where(tri, gc_h - gr_h, 0.0)), 0.0)
        egc = jnp.exp(gc_h)
        terms.append(dict(d=d, j=j,
                          L=jnp.where(strict, beta_h * kk * decay, 0.0),
                          a_in=jnp.where(tri, qk * decay, 0.0),
                          vb=v * beta_h, kbg=k * (beta_h * egc), qg=q * egc,
                          kg=k * jnp.exp(gl_h - gc_h)))
    return terms


def _dn_prep(xb, ab, conv_w, a_log, dt_bias, B, T):
    n = xb.shape[0]
    nc = T // CHUNK
    cps = _pick_tile(nc, DN_CHUNKS_PER_STEP)
    ns = nc // cps
    rows = cps * CHUNK
    nblk = n // HALO
    bps = rows // HALO
    W3 = xb.shape[1]
    nh2 = 2 * H_B
    abT3 = ab.reshape(n // CHUNK, CHUNK, 16).transpose(0, 2, 1)
    al = a_log.reshape(1, nh2)
    dt = dt_bias.reshape(1, nh2)
    st = lambda b, c: b * ns + c
    idx5 = lambda b, c: (0, st(b, c), 0, 0, 0)
    const = lambda b, c: (0, 0)
    return pl.pallas_call(
        _dn_prep_kernel, grid=(B, ns),
        in_specs=[pl.BlockSpec((rows, W3), lambda b, c: (st(b, c), 0)),
                  pl.BlockSpec((HALO, W3), lambda b, c: (jnp.maximum(st(b, c) * bps - 1, 0), 0)),
                  pl.BlockSpec((HALO, W3), lambda b, c: (jnp.minimum((st(b, c) + 1) * bps, nblk - 1), 0)),
                  pl.BlockSpec((rows, 16), lambda b, c: (st(b, c), 0)),
                  pl.BlockSpec((cps, 16, CHUNK), lambda b, c: (st(b, c), 0, 0)),
                  pl.BlockSpec(conv_w.shape, const),
                  pl.BlockSpec((1, nh2), const), pl.BlockSpec((1, nh2), const),
                  pl.BlockSpec((nh2, 1), const), pl.BlockSpec((nh2, 1), const)],
        out_specs=[pl.BlockSpec((2, cps, H_B, CHUNK, D_B), idx5),
                   pl.BlockSpec((2, cps, H_B, 2 * CHUNK, D_B), idx5),
                   pl.BlockSpec((2, cps, H_B, CHUNK, D_B), idx5),
                   pl.BlockSpec((2, cps, H_B, CHUNK, CHUNK), idx5),
                   pl.BlockSpec((cps, nh2, LANES), lambda b, c: (st(b, c), 0, 0))],
        out_shape=[jax.ShapeDtypeStruct((2, B * nc, H_B, CHUNK, D_B), F32),
                   jax.ShapeDtypeStruct((2, B * nc, H_B, 2 * CHUNK, D_B), BF16),
                   jax.ShapeDtypeStruct((2, B * nc, H_B, CHUNK, D_B), BF16),
                   jax.ShapeDtypeStruct((2, B * nc, H_B, CHUNK, CHUNK), BF16),
                   jax.ShapeDtypeStruct((B * nc, nh2, LANES), F32)],
        scratch_shapes=[pltpu.VMEM((rows + 2 * HALO, W3), F32)],
        compiler_params=_cparams("parallel", "parallel"), name="dn_prep",
    )(xb, xb, xb, ab, abT3, conv_w, al, dt, al.reshape(nh2, 1), dt.reshape(nh2, 1))


def _dn_scan_kernel(*refs):
    s0_ref = refs[0]
    per_dir = (refs[1:6], refs[6:11])
    of_ref, ob_ref, sfin_ref, s_ref = refs[11:15]
    c = pl.program_id(1)
    nc = pl.num_programs(1)

    @pl.when(c == 0)
    def _():
        s_ref[...] = s0_ref[0]

    chains = [(d, h) for d in range(2) for h in range(H_B)]
    states = [s_ref[d, h] for d, h in chains]
    rs = [_dot(per_dir[d][1][0, 0, h], s.astype(BF16)) for (d, h), s in zip(chains, states)]
    vns = [(per_dir[d][0][0, 0, h] - r[0:CHUNK]).astype(BF16) for (d, h), r in zip(chains, rs)]
    avs = [_dot(per_dir[d][3][0, 0, h], vn) for (d, h), vn in zip(chains, vns)]
    dss = [_dot_tn(per_dir[d][2][0, 0, h], vn) for (d, h), vn in zip(chains, vns)]
    for i, (d, h) in enumerate(chains):
        o_ref = of_ref if d == 0 else ob_ref
        o_ref[:, h * D_B:(h + 1) * D_B] = rs[i][CHUNK:] + avs[i]
        eg = per_dir[d][4][0, d * H_B + h:d * H_B + h + 1, :]
        s_ref[d, h] = states[i] * eg + dss[i]

    @pl.when(c == nc - 1)
    def _():
        sfin_ref[0] = s_ref[...]


def _dn_scan(prep, s0, B, T):
    u, wq, kg, a, eg = prep
    nc = T // CHUNK
    R = DN_ROWS
    nh2 = 2 * H_B

    def dir_specs(d):
        cb = (lambda b, c: b * nc + c) if d == 0 else (lambda b, c: b * nc + nc - 1 - c)
        idx5 = lambda b, c: (d, cb(b, c), 0, 0, 0)
        return [pl.BlockSpec((1, 1, H_B, CHUNK, D_B), idx5),
                pl.BlockSpec((1, 1, H_B, 2 * CHUNK, D_B), idx5),
                pl.BlockSpec((1, 1, H_B, CHUNK, D_B), idx5),
                pl.BlockSpec((1, 1, H_B, CHUNK, CHUNK), idx5),
                pl.BlockSpec((1, nh2, LANES), lambda b, c: (cb(b, c), 0, 0))]

    st = (1, 2, H_B, D_B, D_B)
    st_spec = pl.BlockSpec(st, lambda b, c: (b, 0, 0, 0, 0))
    return pl.pallas_call(
        _dn_scan_kernel, grid=(B, nc),
        in_specs=[st_spec] + dir_specs(0) + dir_specs(1),
        out_specs=[pl.BlockSpec((CHUNK, W_HEADS), lambda b, c: (b * nc + c, 0)),
                   pl.BlockSpec((CHUNK, W_HEADS), lambda b, c: (b * nc + nc - 1 - c, 0)),
                   st_spec],
        out_shape=[jax.ShapeDtypeStruct((B * T, W_HEADS), F32), jax.ShapeDtypeStruct((B * T, W_HEADS), F32),
                   jax.ShapeDtypeStruct((B,) + st[1:], F32)],
        scratch_shapes=[pltpu.VMEM(st[1:], F32)],
        compiler_params=_cparams("parallel", "arbitrary"), name="dn_scan",
    )(s0, u, wq, kg, a, eg, u, wq, kg, a, eg)


def _merge_kernel(x_ref, mod_ref, g1_ref, g2_ref, oa_ref, of_ref, ob_ref, zb_ref, gon_ref, oc_ref,
                  wg_ref, wa_ref, wb_ref, wc_ref, wo_ref, wrT_ref, x1_ref, h2_ref, lgT_ref):
    D = x_ref.shape[1]
    x = x_ref[...]
    sh1, sc1, gt1, sh2, sc2 = [mod_ref[0, j:j + 1, :] for j in range(5)]
    hb = (_rms(x, g1_ref[...]) * (1.0 + sc1) + sh1).astype(BF16)
    od = of_ref[...] + ob_ref[...]
    z = zb_ref[...]
    gon = gon_ref[...]
    obs = []
    for h in range(H_B):
        cs = slice(h * D_B, (h + 1) * D_B)
        obs.append(_rms(od[:, cs], gon) * _silu(z[:, cs]))
    ob = jnp.concatenate(obs, axis=1).astype(BF16)

    def gate(j):
        return _sigmoid(_dot(hb, wg_ref[:, j * D:(j + 1) * D]))

    merged = (gate(0) * _dot(oa_ref[...], wa_ref[...])
              + gate(1) * _dot(ob, wb_ref[...])
              + gate(2) * _dot(oc_ref[...], wc_ref[...]))
    x1 = x + gt1 * _dot(merged.astype(BF16), wo_ref[...])
    x1_ref[...] = x1
    h2 = _rms(x1, g2_ref[...]) * (1.0 + sc2) + sh2
    h2_ref[...] = h2
    E = lgT_ref.shape[0]
    h2h = h2.astype(BF16)
    h2l = (h2 - h2h.astype(F32)).astype(BF16)
    r1 = _dot_nt(wrT_ref[...], h2h)
    lgT_ref[...] = r1[:E] + (r1[E:] + _dot_nt(wrT_ref[0:E, :], h2l))


def _merge(x, mod5, g1, g2, oa, o_f, o_b, zb, g_onorm, oc, w_g, w_a, w_b, w_c, w_o, w_rT, T):
    n, D = x.shape
    tm = _pick_tile(T, 256)
    nt = T // tm
    W = W_HEADS
    E = w_rT.shape[0] // 2
    row = lambda i: (i, 0)
    const = lambda i: (0, 0)
    full = lambda a: pl.BlockSpec(a.shape, const)
    return pl.pallas_call(
        _merge_kernel, grid=(n // tm,),
        in_specs=[pl.BlockSpec((tm, D), row), pl.BlockSpec((1, 8, D), lambda i: (i // nt, 0, 0)),
                  pl.BlockSpec((1, D), const), pl.BlockSpec((1, D), const),
                  pl.BlockSpec((tm, W), row), pl.BlockSpec((tm, W), row), pl.BlockSpec((tm, W), row),
                  pl.BlockSpec((tm, W), row), pl.BlockSpec((1, D_B), const), pl.BlockSpec((tm, W), row),
                  full(w_g), full(w_a), full(w_b), full(w_c), full(w_o), full(w_rT)],
        out_specs=[pl.BlockSpec((tm, D), row), pl.BlockSpec((tm, D), row), pl.BlockSpec((E, tm), lambda i: (0, i))],
        out_shape=[jax.ShapeDtypeStruct((n, D), F32), jax.ShapeDtypeStruct((n, D), F32),
                   jax.ShapeDtypeStruct((E, n), F32)],
        compiler_params=_cparams("parallel"), name="merge",
    )(x, mod5, g1, g2, oa, o_f, o_b, zb, g_onorm, oc, w_g, w_a, w_b, w_c, w_o, w_rT)


def _route_kernel(lg_ref, idx_ref, w_ref, rc_ref, *, cap):
    E, R, _ = lg_ref.shape
    lg = lg_ref[...]
    m = jnp.max(lg, axis=0, keepdims=True)
    ex = jnp.exp(lg - m)
    aff = ex / jnp.sum(ex, axis=0, keepdims=True)
    bits = pltpu.bitcast(aff, I32)

    def count(mask):
        t = jnp.sum(jnp.where(mask, 1.0, 0.0), axis=2, keepdims=True)
        return jnp.sum(t, axis=1, keepdims=True)

    def bisect(_, carry):
        lo, hi = carry
        mid = lo + lax.shift_right_logical(hi - lo, 1)
        ge = count(bits >= mid) >= cap
        return jnp.where(ge, mid, lo), jnp.where(ge, hi, mid)

    lo0 = jnp.zeros((E, 1, 1), I32)
    hi0 = jnp.full((E, 1, 1), 0x7F800000, I32)
    thr, _ = lax.fori_loop(0, 31, bisect, (lo0, hi0))
    gt = bits > thr
    eq = bits == thr
    need = cap - count(gt)

    lane_r = _iota((LANES, LANES), 0)
    lane_c = _iota((LANES, LANES), 1)
    u_lane = (lane_r <= lane_c).astype(BF16)
    row_r = _iota((R, R), 0)
    row_c = _iota((R, R), 1)
    sl_row = (row_c < row_r).astype(BF16)
    u_row = (row_r <= row_c).astype(BF16)
    ones8 = jnp.ones((8, LANES), BF16)
    eye_l = (lane_r == lane_c).astype(BF16)

    def prefix(masks):
        ws = [_dot(mk.astype(BF16), u_lane) for mk in masks]
        tots = [(jnp.zeros((R, LANES), F32) + w[:, LANES - 1:LANES]).astype(BF16) for w in ws]
        return [w + _dot(sl_row, tot) for w, tot in zip(ws, tots)], ws

    gtf = jnp.where(gt, 1.0, 0.0)
    eqf = jnp.where(eq, 1.0, 0.0)
    experts = range(E)
    pes = prefix([eqf[e] for e in experts])[0]
    sels = [jnp.maximum(gtf[e], jnp.where(pes[e] <= need[e], eqf[e], 0.0)) for e in experts]
    ws = prefix(sels)[1]
    rts = [_dot_nt(ones8, s.astype(BF16)) for s in sels]
    rcs = [_dot(rt.astype(BF16), u_row) for rt in rts]
    for e in experts:
        w_ref[e] = ws[e]
        rc_ref[e] = rcs[e]

    nblk = cap // LANES
    G = _pick_tile(nblk, ROUTE_BLOCKS_PER_STEP)
    ngrp = nblk // G
    s_base = _iota((G * LANES, 1), 0).astype(F32)

    def slot_group(i, _):
        e = i // ngrp
        jg = i % ngrp
        w_e = w_ref[e]
        rc_row = rc_ref[e][0:1, :]
        s_col = s_base + jnp.asarray(jg * (G * LANES), F32)
        below = rc_row <= s_col
        rho = jnp.sum(jnp.where(below, 1.0, 0.0), axis=1, keepdims=True)
        before = jnp.max(jnp.where(below, rc_row, 0.0), axis=1, keepdims=True)
        onehot = (_iota((G * LANES, R), 1).astype(F32) == rho).astype(BF16)
        wg = _dot(onehot, w_e.astype(BF16))
        lam = jnp.sum(jnp.where(wg <= s_col - before, 1.0, 0.0), axis=1, keepdims=True)
        blocks = [slice(g * LANES, (g + 1) * LANES) for g in range(G)]
        rho_l = [_dot_nt(ones8, (eye_l * rho[b]).astype(BF16)) for b in blocks]
        lam_l = [_dot_nt(ones8, (eye_l * lam[b]).astype(BF16)) for b in blocks]
        for g in range(G):
            idx_ref[i * G + g] = (rho_l[g][0:1, :] * LANES + lam_l[g][0:1, :]).astype(I32)
        return 0

    lax.fori_loop(0, E * ngrp, slot_group, 0)


def _route(logits_t, cap):
    E, n = logits_t.shape
    R = n // LANES
    nblk = cap // LANES
    idx = pl.pallas_call(
        functools.partial(_route_kernel, cap=cap),
        out_shape=jax.ShapeDtypeStruct((E * nblk, 1, LANES), I32),
        scratch_shapes=[pltpu.VMEM((E, R, LANES), F32), pltpu.VMEM((E, 8, R), F32)],
        compiler_params=pltpu.CompilerParams(vmem_limit_bytes=VMEM_LIMIT_BYTES), name="route",
    )(logits_t.reshape(E, R, LANES))
    return idx.reshape(E, cap)


def _moe_kernel(idx_ref, idxn_ref, h_hbm, wr_ref, wg_ref, wu_ref, wd_ref, acc_in, acc_hbm,
                xbuf, mbuf, sem_x, sem_m, sem_s, *, nt):
    del acc_in
    e = pl.program_id(0)
    k = pl.program_id(1)
    total = pl.num_programs(0) * nt
    step = e * nt + k
    slot = step % 2
    tm = xbuf.shape[1]

    def row_copies(src_of, dst_of, sem, ids):
        def start(r, _):
            t = ids[0, 0, r]
            pltpu.make_async_copy(src_of(t, r), dst_of(t, r), sem).start()
            return 0
        lax.fori_loop(0, tm, start, 0, unroll=8)

    def wait_rows(sem, buf):
        pltpu.make_async_copy(buf, buf, sem).wait()

    def gather_x(ids, s):
        row_copies(lambda t, r: h_hbm.at[pl.ds(t, 1)], lambda t, r: xbuf.at[s, pl.ds(r, 1)], sem_x.at[s], ids)

    @pl.when(step == 0)
    def _():
        gather_x(idx_ref, 0)

    wait_rows(sem_x.at[slot], xbuf.at[slot])

    @pl.when(step + 1 < total)
    def _():
        gather_x(idxn_ref, 1 - slot)

    @pl.when(step > 0)
    def _():
        wait_rows(sem_s, mbuf)
    row_copies(lambda t, r: acc_hbm.at[pl.ds(t, 1)], lambda t, r: mbuf.at[pl.ds(r, 1)], sem_m, idx_ref)

    xe = xbuf[slot].astype(BF16)
    hid = _silu(_dot(xe, wg_ref[0])) * _dot(xe, wu_ref[0])
    lg = _dot(xe, wr_ref[...])
    lane = _iota(lg.shape, 1)
    lg = jnp.where(lane < N_EXPERTS, lg, NEG)
    ex = jnp.exp(lg - jnp.max(lg, axis=-1, keepdims=True))
    aff = ex / jnp.sum(ex, axis=-1, keepdims=True)
    gate = jnp.sum(jnp.where(lane == e, aff, 0.0), axis=-1, keepdims=True)
    ye = _dot(hid.astype(BF16), wd_ref[0]) * gate

    wait_rows(sem_m, mbuf)
    mbuf[...] = mbuf[...] + ye
    row_copies(lambda t, r: mbuf.at[pl.ds(r, 1)], lambda t, r: acc_hbm.at[pl.ds(t, 1)], sem_s, idx_ref)

    @pl.when(step == total - 1)
    def _():
        wait_rows(sem_s, mbuf)


def _moe(h2, idx, w_router_pad, w_gate, w_up, w_down, acc):
    n, D = h2.shape
    E, cap = idx.shape
    F = w_gate.shape[-1]
    tm = _pick_tile(cap, 256)
    nt = cap // tm
    idx2 = idx.reshape(E * nt, 1, tm)
    last = E * nt - 1
    smem_blk = lambda f: pl.BlockSpec((1, 1, tm), f, memory_space=pltpu.SMEM)
    return pl.pallas_call(
        functools.partial(_moe_kernel, nt=nt), grid=(E, nt),
        in_specs=[smem_blk(lambda e, k: (e * nt + k, 0, 0)),
                  smem_blk(lambda e, k: (jnp.minimum(e * nt + k + 1, last), 0, 0)),
                  pl.BlockSpec(memory_space=pl.ANY),
                  pl.BlockSpec(w_router_pad.shape, lambda e, k: (0, 0)),
                  pl.BlockSpec((1, D, F), lambda e, k: (e, 0, 0)),
                  pl.BlockSpec((1, D, F), lambda e, k: (e, 0, 0)),
                  pl.BlockSpec((1, F, D), lambda e, k: (e, 0, 0)),
                  pl.BlockSpec(memory_space=pl.ANY)],
        out_specs=pl.BlockSpec(memory_space=pl.ANY),
        out_shape=jax.ShapeDtypeStruct((n, D), F32),
        input_output_aliases={7: 0},
        scratch_shapes=[pltpu.VMEM((2, tm, D), F32), pltpu.VMEM((tm, D), F32),
                        pltpu.SemaphoreType.DMA((2,)), pltpu.SemaphoreType.DMA(()), pltpu.SemaphoreType.DMA(())],
        compiler_params=_cparams("arbitrary", "arbitrary"), name="moe",
    )(idx2, idx2, h2, w_router_pad, w_gate, w_up, w_down, acc)


def _resid_kernel(x_ref, m_ref, gt_ref, g_ref, o_ref, *, final):
    x2 = x_ref[...] + gt_ref[0] * m_ref[...]
    o_ref[...] = _rms(x2, g_ref[...]) if final else x2


def _resid(x1, m, gt2, g_final, T, final):
    n, D = x1.shape
    tm = _pick_tile(T, 512)
    nt = T // tm
    row = lambda i: (i, 0)
    return pl.pallas_call(
        functools.partial(_resid_kernel, final=final), grid=(n // tm,),
        in_specs=[pl.BlockSpec((tm, D), row), pl.BlockSpec((tm, D), row),
                  pl.BlockSpec((1, 1, D), lambda i: (i // nt, 0, 0)), pl.BlockSpec((1, D), lambda i: (0, 0))],
        out_specs=pl.BlockSpec((tm, D), row), out_shape=jax.ShapeDtypeStruct((n, D), F32),
        compiler_params=_cparams("parallel"), name="resid",
    )(x1, m, gt2, g_final)


def _lambda_init(layer):
    return 0.8 - 0.6 * math.exp(-0.3 * layer)


def _hi_lo_rows(w):
    hi = w.astype(BF16)
    return jnp.concatenate([hi, (w - hi.astype(F32)).astype(BF16)], axis=0)


def _prep_weights(p, l):
    W = W_HEADS
    D = p['w_in'].shape[1]
    w_in = p['w_in'][l]
    nab = 4 * H_B
    o_ab = 3 * W + 4 * W
    o_c = o_ab + nab
    o_g = o_c + 3 * W
    w_main = jnp.concatenate([w_in[:, :o_ab], w_in[:, o_c:o_g]], axis=1).astype(BF16)
    w_ab = w_in[:, o_ab:o_c].astype(BF16)
    E = p['w_router'].shape[-1]
    w_r = p['w_router'][l]
    return dict(
        w_main=w_main, w_ab=w_ab,
        w_g=w_in[:, o_g:].astype(BF16),
        w_a=p['w_br_a'][l].astype(BF16), w_b=p['w_br_b'][l].astype(BF16), w_c=p['w_br_c'][l].astype(BF16),
        w_o=p['w_out'][l].astype(BF16),
        w_rT=_hi_lo_rows(w_r.T),
        w_r_pad=jnp.pad(w_r, ((0, 0), (0, LANES - E))).astype(BF16),
        w_gate=p['w_e_gate'][l].astype(BF16), w_up=p['w_e_up'][l].astype(BF16), w_down=p['w_e_down'][l].astype(BF16),
        g1=p['g_norm1'][l][None], g2=p['g_norm2'][l][None],
        conv_w=p['conv_w'][l], a_log=p['a_log'][l], dt_bias=p['dt_bias'][l],
        g_onorm=p['g_onorm'][l][None], lam_qk=p['lam_qk'][l], g_subln=p['g_subln'][l][None], rpb=p['rpb'][l],
    )


def _trunk_layer(x, mod, lw, l, B, T, g_final, final, ctx=None):
    n, D = x.shape
    lam_init = _lambda_init(l)
    latent = ctx is not None
    sh1, sc1 = mod[:, 0:1], mod[:, 1:2]
    outs = _inproj(x, sc1, sh1, lw['g1'], lw['w_main'], lw['w_ab'], T,
                   _rope_tables(T) if latent else None, emit_kv=not latent)
    qa, ka, va, xb, zb, qc, kc, vc, ab = outs[:9]
    prep = _dn_prep(xb, ab, lw['conv_w'], lw['a_log'], lw['dt_bias'], B, T)
    if latent:
        ck_a, cv_a, ck_c, cv_c, c_state = ctx
        oa = _diff_attn(qa, ka, va, ck_a, cv_a, lw['lam_qk'], lw['g_subln'], B, T, lam_init)
        o_f, o_b, _ = _dn_scan(prep, c_state, B, T)
        oc = _na_attn(qc, kc, vc, ck_c, cv_c, lw['rpb'], B, T)
        new_ctx = None
    else:
        oa = _diff_attn(qa, ka, va, None, None, lw['lam_qk'], lw['g_subln'], B, T, lam_init)
        s0 = jnp.zeros((B, 2, H_B, D_B, D_B), F32)
        o_f, o_b, s_fin = _dn_scan(prep, s0, B, T)
        oc = _dense_attn(qc, kc, vc, B, T)
        new_ctx = tuple(a.reshape(B, T, 4, 128) for a in outs[9:13]) + (s_fin,)
    mod5 = jnp.pad(mod[:, :5], ((0, 0), (0, 3), (0, 0)))
    x1, h2, lg_t = _merge(x, mod5, lw['g1'], lw['g2'], oa, o_f, o_b, zb, lw['g_onorm'], oc,
                          lw['w_g'], lw['w_a'], lw['w_b'], lw['w_c'], lw['w_o'], lw['w_rT'], T)
    cap = EC_FACTOR * n // N_EXPERTS
    idx = _route(lg_t, cap)
    m = _moe(h2, idx, lw['w_r_pad'], lw['w_gate'], lw['w_up'], lw['w_down'], jnp.zeros((n, D), F32))
    x2 = _resid(x1, m, mod[:, 5:6], g_final, T, final)
    return x2, new_ctx


def kernel(x_prompt, x_sample, c, cache_diff_k, cache_diff_v, cache_na_k, cache_na_v, state_delta, c_ctx, w_ada, b_ada, g_norm1, g_norm2, w_in, conv_w, a_log, dt_bias, g_onorm, lam_qk, g_subln, rpb, w_br_a, w_br_b, w_br_c, w_out, w_router, w_e_gate, w_e_up, w_e_down, g_final):
    p = dict(w_in=w_in, conv_w=conv_w, a_log=a_log, dt_bias=dt_bias, g_onorm=g_onorm, lam_qk=lam_qk,
             g_subln=g_subln, rpb=rpb, w_br_a=w_br_a, w_br_b=w_br_b, w_br_c=w_br_c, w_out=w_out,
             w_router=w_router, w_e_gate=w_e_gate, w_e_up=w_e_up, w_e_down=w_e_down,
             g_norm1=g_norm1, g_norm2=g_norm2)
    depth = w_in.shape[0]
    Bp, Tp, D = x_prompt.shape
    Bs, Ts, _ = x_sample.shape
    gf = g_final[None]

    nrow = 1 + Bs
    cond = jnp.pad(jnp.concatenate([c_ctx[None], c], axis=0), ((0, -nrow % 8), (0, 0)))
    mod = _adaln(cond, w_ada, b_ada).reshape(depth, -1, 6, D)
    layers = [_prep_weights(p, l) for l in range(depth)]

    xp = x_prompt.reshape(Bp * Tp, D)
    new_ctx = []
    for l in range(depth):
        mod_ctx = jnp.broadcast_to(mod[l, 0:1], (Bp, 6, D))
        xp, nc = _trunk_layer(xp, mod_ctx, layers[l], l, Bp, Tp, gf, l == depth - 1)
        new_ctx.append(nc)

    xs = x_sample.reshape(Bs * Ts, D)
    for l in range(depth):
        Lc = cache_diff_k.shape[2]
        flat = lambda a: a[:, l].reshape(Bs * Lc, W_HEADS).astype(BF16)
        ctx = (flat(cache_diff_k), flat(cache_diff_v), flat(cache_na_k), flat(cache_na_v), state_delta[:, l])
        xs, _ = _trunk_layer(xs, mod[l, 1:1 + Bs], layers[l], l, Bs, Ts, gf, l == depth - 1, ctx)

    stack = lambda j: jnp.stack([nc[j] for nc in new_ctx], axis=1)
    return (xp.reshape(Bp, Tp, D), xs.reshape(Bs, Ts, D), stack(0), stack(1), stack(2), stack(3), stack(4))
```

```python
import functools
import math

import jax
import jax.numpy as jnp
from jax import lax
from jax.experimental import pallas as pl
from jax.experimental.pallas import tpu as pltpu

F32 = jnp.float32
BF16 = jnp.bfloat16
I32 = jnp.int32

H_A, D_A = 4, 64
H_B, D_B = 4, 128
H_C, D_C = 4, 128
CONV_K = 5
CHUNK = 64
GRID_W = 64
WIN_R, WIN_C = 8, 16
N_EXPERTS = 16
EC_FACTOR = 2
ROPE_BASE = 10000.0
EPS = 1e-6
NEG = -1e30
W_HEADS = 512
ATTN_KEY_CHUNK = 1024
LOG2E = 1.4426950408889634
ROUTE_BLOCKS_PER_STEP = 4
MOE_ISSUE_CHUNKS = 4

LANES = 128
VMEM_LIMIT_BYTES = 56 * 1024 * 1024


def _cparams(*sem):
    return pltpu.CompilerParams(dimension_semantics=sem, vmem_limit_bytes=VMEM_LIMIT_BYTES)


def _dot(a, b):
    return jnp.dot(a, b, preferred_element_type=F32)


def _dot_nt(a, b):
    return lax.dot_general(a, b, (((1,), (1,)), ((), ())), preferred_element_type=F32)


def _dot_tn(a, b):
    return lax.dot_general(a, b, (((0,), (0,)), ((), ())), preferred_element_type=F32)


def _dot_f32(a, b):
    return jnp.dot(a, b, preferred_element_type=F32, precision=lax.Precision.HIGHEST)


def _sigmoid(x):
    return 1.0 / (1.0 + jnp.exp(-x))


def _silu(x):
    return x * _sigmoid(x)


def _softplus(x):
    return jnp.maximum(x, 0.0) + jnp.log(1.0 + jnp.exp(-jnp.abs(x)))


def _rms(x, g):
    return x * lax.rsqrt(jnp.mean(x * x, axis=-1, keepdims=True) + EPS) * g


def _iota(shape, dim):
    return lax.broadcasted_iota(I32, shape, dim)


def _pick_tile(n, pref):
    t = min(n, pref)
    while n % t:
        t //= 2
    return t


def _adaln_kernel(c_ref, w_ref, b_ref, o_ref):
    c = c_ref[...]
    s = _silu(c).astype(BF16)
    o_ref[0] = _dot(s, w_ref[0].astype(BF16)) + b_ref[0]


def _adaln(cond, w_ada, b_ada):
    L, D, N = w_ada.shape
    R = cond.shape[0]
    tn = _pick_tile(N, 1536)
    return pl.pallas_call(
        _adaln_kernel,
        grid=(L, N // tn),
        in_specs=[pl.BlockSpec((R, D), lambda l, j: (0, 0)),
                  pl.BlockSpec((1, D, tn), lambda l, j: (l, 0, j)),
                  pl.BlockSpec((1, 1, tn), lambda l, j: (l, 0, j))],
        out_specs=pl.BlockSpec((1, R, tn), lambda l, j: (l, 0, j)),
        out_shape=jax.ShapeDtypeStruct((L, R, N), F32),
        compiler_params=_cparams("parallel", "parallel"),
        name="adaln",
    )(cond, w_ada, b_ada.reshape(L, 1, N))


def _inproj_kernel(*refs, rope, emit_kv):
    it = iter(refs)
    x_ref, sc_ref, sh_ref, g_ref, w_ref, wab_ref = [next(it) for _ in range(6)]
    if rope:
        cos_ref, sa_ref, sb_ref = [next(it) for _ in range(3)]
    qa_ref, ka_ref, va_ref, xb_ref, zb_ref, qc_ref, kc_ref, vc_ref, ab_ref = [next(it) for _ in range(9)]
    if emit_kv:
        ka32_ref, va32_ref, kc32_ref, vc32_ref = [next(it) for _ in range(4)]

    x = x_ref[...]
    h = _rms(x, g_ref[...]) * (1.0 + sc_ref[0]) + sh_ref[0]
    hb = h.astype(BF16)
    W = W_HEADS

    def proj(j, n=1):
        return _dot(hb, w_ref[:, j * W:(j + n) * W])

    def roped(v):
        if not rope:
            return v
        n = v.shape[-1]
        return v * cos_ref[...] + pltpu.roll(v, n - D_A // 4, 1) * sa_ref[...] + pltpu.roll(v, D_A // 4, 1) * sb_ref[...]

    qa = proj(0)
    qa_ref[...] = (roped(qa) * (D_A ** -0.5 * LOG2E)).astype(BF16)
    ka = proj(1)
    ka_ref[...] = roped(ka).astype(BF16)
    va = proj(2)
    va_ref[...] = va.astype(BF16)
    xb_ref[...] = proj(3, 3)
    zb_ref[...] = proj(6)
    qc_ref[...] = proj(7).astype(BF16)
    kc = proj(8)
    kc_ref[...] = kc.astype(BF16)
    vc = proj(9)
    vc_ref[...] = vc.astype(BF16)
    ab_ref[...] = _dot(hb, wab_ref[...])
    if emit_kv:
        ka32_ref[...] = ka
        va32_ref[...] = va
        kc32_ref[...] = kc
        vc32_ref[...] = vc


def _inproj(x, sc, sh, g, w_main, w_ab, T, rope_tabs, emit_kv):
    n, D = x.shape
    tm = _pick_tile(T, 512)
    nt = T // tm
    W = W_HEADS
    rope = rope_tabs is not None
    row = lambda i: (i, 0)
    per_b = lambda i: (i // nt, 0, 0)
    const = lambda i: (0, 0)
    in_specs = [pl.BlockSpec((tm, D), row),
                pl.BlockSpec((1, 1, D), per_b), pl.BlockSpec((1, 1, D), per_b),
                pl.BlockSpec((1, D), const),
                pl.BlockSpec(w_main.shape, const), pl.BlockSpec(w_ab.shape, const)]
    args = [x, sc, sh, g, w_main, w_ab]
    if rope:
        in_specs += [pl.BlockSpec((tm, W), lambda i: (i % nt, 0))] * 3
        args += list(rope_tabs)
    bf = lambda: jax.ShapeDtypeStruct((n, W), BF16)
    f32 = lambda w: jax.ShapeDtypeStruct((n, w), F32)
    out_shape = [bf(), bf(), bf(), f32(3 * W), f32(W), bf(), bf(), bf(), f32(16)]
    out_specs = [pl.BlockSpec((tm, W), row)] * 3 + [pl.BlockSpec((tm, 3 * W), row), pl.BlockSpec((tm, W), row)] \
        + [pl.BlockSpec((tm, W), row)] * 3 + [pl.BlockSpec((tm, 16), row)]
    if emit_kv:
        out_shape += [f32(W)] * 4
        out_specs += [pl.BlockSpec((tm, W), row)] * 4
    return pl.pallas_call(
        functools.partial(_inproj_kernel, rope=rope, emit_kv=emit_kv),
        grid=(n // tm,), in_specs=in_specs, out_specs=out_specs, out_shape=out_shape,
        compiler_params=_cparams("parallel"), name="inproj",
    )(*args)


def _rope_tables(T):
    quarter = D_A // 4
    inv = 1.0 / (ROPE_BASE ** (jnp.arange(quarter, dtype=F32) / quarter))
    t = jnp.arange(T)
    pos = jnp.stack([t // GRID_W, t % GRID_W], axis=-1).astype(F32)
    ang = pos[:, :, None] * inv
    cos, sin = jnp.cos(ang), jnp.sin(ang)
    cos64 = jnp.concatenate([cos[:, 0], cos[:, 0], cos[:, 1], cos[:, 1]], axis=-1)
    zero = jnp.zeros_like(sin[:, 0])
    sa64 = jnp.concatenate([-sin[:, 0], zero, -sin[:, 1], zero], axis=-1)
    sb64 = jnp.concatenate([zero, sin[:, 0], zero, sin[:, 1]], axis=-1)
    rep = W_HEADS // D_A
    return tuple(jnp.tile(a, (1, rep)) for a in (cos64, sa64, sb64))


def _diff_attn_kernel(*refs, has_cache, lam_init):
    if has_cache:
        q_ref, k_ref, v_ref, kc_ref, vc_ref, lq_ref, g_ref, o_ref = refs
    else:
        q_ref, k_ref, v_ref, lq_ref, g_ref, o_ref = refs
    q = q_ref[...]
    lane = _iota(q.shape, 1)
    zero = jnp.zeros_like(q)
    qs = (jnp.where(lane < D_A, q, zero), jnp.where(lane >= D_A, q, zero))
    lq = lq_ref[...]
    lam = (jnp.exp(jnp.sum(lq[0:1] * lq[1:2], axis=-1, keepdims=True))
           - jnp.exp(jnp.sum(lq[2:3] * lq[3:4], axis=-1, keepdims=True)) + lam_init)
    hd = q.shape[1]

    def with_ones(v):
        one = jnp.where(_iota(v.shape, 1) == 0, 1.0, 0.0).astype(BF16)
        return jnp.concatenate([v, one], axis=1)

    T = k_ref.shape[0]
    tk = _pick_tile(T, ATTN_KEY_CHUNK)
    chunks = [(k_ref[j * tk:(j + 1) * tk, :], with_ones(v_ref[j * tk:(j + 1) * tk, :])) for j in range(T // tk)]
    if has_cache:
        chunks.append((kc_ref[...], with_ones(vc_ref[...])))
    tq = q.shape[0]
    outs = []
    for comp in range(2):
        m = jnp.full((tq, 1), NEG, F32)
        acc = jnp.zeros((tq, 2 * hd), F32)
        for kj, vxj in chunks:
            s = _dot_nt(qs[comp], kj)
            m_new = jnp.maximum(m, jnp.max(s, axis=-1, keepdims=True))
            acc = acc * jnp.exp2(m - m_new) + _dot(jnp.exp2(s - m_new).astype(BF16), vxj)
            m = m_new
        outs.append(acc[:, :hd] * (1.0 / acc[:, hd:hd + 1]))
    o = outs[0] - lam * outs[1]
    o_ref[...] = (_rms(o, g_ref[...]) * (1.0 - lam_init)).astype(BF16)


def _diff_attn(q, k, v, cache_k, cache_v, lam_qk, g_sub, B, T, lam_init):
    n = q.shape[0]
    tq = _pick_tile(T, 512)
    nq = T // tq
    has_cache = cache_k is not None
    hd = 2 * D_A
    in_specs = [pl.BlockSpec((tq, hd), lambda b, h, i: (b * nq + i, h)),
                pl.BlockSpec((T, hd), lambda b, h, i: (b, h)),
                pl.BlockSpec((T, hd), lambda b, h, i: (b, h))]
    args = [q, k, v]
    if has_cache:
        Lc = cache_k.shape[0] // B
        in_specs += [pl.BlockSpec((Lc, hd), lambda b, h, i: (b, h))] * 2
        args += [cache_k, cache_v]
    in_specs += [pl.BlockSpec(lam_qk.shape, lambda b, h, i: (0, 0)), pl.BlockSpec((1, hd), lambda b, h, i: (0, 0))]
    args += [lam_qk, g_sub]
    return pl.pallas_call(
        functools.partial(_diff_attn_kernel, has_cache=has_cache, lam_init=lam_init),
        grid=(B, H_A, nq), in_specs=in_specs,
        out_specs=pl.BlockSpec((tq, hd), lambda b, h, i: (b * nq + i, h)),
        out_shape=jax.ShapeDtypeStruct((n, W_HEADS), BF16),
        compiler_params=_cparams("parallel", "parallel", "parallel"), name="diff_attn",
    )(*args)


def _softmax_pv(heads):
    def fold(xs, op):
        out = xs[0]
        for x in xs[1:]:
            out = op(out, x)
        return out

    ms = [fold([jnp.max(s, axis=-1, keepdims=True) for s, _ in parts], jnp.maximum) for parts in heads]
    ps = [[jnp.exp(s - m) for s, _ in parts] for parts, m in zip(heads, ms)]
    rs = [1.0 / fold([jnp.sum(p, axis=-1, keepdims=True) for p in pp], jnp.add) for pp in ps]
    pbs = [[(p * r).astype(BF16) for p in pp] for pp, r in zip(ps, rs)]
    return [fold([_dot(p, v) for p, (_, v) in zip(pp, parts)], jnp.add) for pp, parts in zip(pbs, heads)]


def _na_kernel(q_ref, k_ref, v_ref, kc_ref, vc_ref, bias_ref, o_ref, *, rows, wr):
    r = pl.program_id(1)
    rs = jnp.clip(r - wr // 2, 0, rows - wr)
    start = pl.multiple_of(rs * GRID_W, GRID_W)
    nloc = wr * GRID_W
    scale = D_C ** -0.5
    cols = [slice(h * D_C, (h + 1) * D_C) for h in range(H_C)]
    qs = [q_ref[:, cs] for cs in cols]
    s_loc = [_dot_nt(q, k_ref[pl.ds(start, nloc), cs]) for q, cs in zip(qs, cols)]
    s_ctx = [_dot_nt(q, kc_ref[:, cs]) for q, cs in zip(qs, cols)]
    heads = [[(s_loc[h] * scale + bias_ref[0, h], v_ref[pl.ds(start, nloc), cols[h]]),
              (s_ctx[h] * scale, vc_ref[:, cols[h]])] for h in range(H_C)]
    for cs, o in zip(cols, _softmax_pv(heads)):
        o_ref[:, cs] = o.astype(BF16)


def _na_bias_table(rpb, rows, wr):
    col = jnp.arange(GRID_W)
    cs = jnp.clip(col - WIN_C // 2, 0, GRID_W - WIN_C)
    col_mask = (col[None, :] >= cs[:, None]) & (col[None, :] < cs[:, None] + WIN_C)
    dc_idx = jnp.clip(col[None, :] - col[:, None] + WIN_C - 1, 0, 2 * WIN_C - 2)
    onehot = (dc_idx[:, :, None] == jnp.arange(2 * WIN_C - 1)).astype(F32)
    base = jnp.einsum('hdj,qkj->hdqk', rpb.astype(F32), onehot, precision=lax.Precision.HIGHEST)
    base = jnp.where(col_mask[None, None], base, NEG)
    tabs = []
    for off in range(wr):
        lo = WIN_R - 1 - off
        tabs.append(base[:, lo:lo + wr].transpose(0, 2, 1, 3).reshape(H_C, GRID_W, wr * GRID_W))
    return jnp.stack(tabs)


def _na_attn(q, k, v, cache_k, cache_v, rpb, B, T):
    n = q.shape[0]
    rows = T // GRID_W
    wr = min(WIN_R, rows)
    Lc = cache_k.shape[0] // B
    bias = _na_bias_table(rpb, rows, wr)
    W = W_HEADS

    def bias_idx(b, r):
        return (r - jnp.clip(r - wr // 2, 0, rows - wr), 0, 0, 0)

    return pl.pallas_call(
        functools.partial(_na_kernel, rows=rows, wr=wr),
        grid=(B, rows),
        in_specs=[pl.BlockSpec((GRID_W, W), lambda b, r: (b * rows + r, 0)),
                  pl.BlockSpec((T, W), lambda b, r: (b, 0)),
                  pl.BlockSpec((T, W), lambda b, r: (b, 0)),
                  pl.BlockSpec((Lc, W), lambda b, r: (b, 0)),
                  pl.BlockSpec((Lc, W), lambda b, r: (b, 0)),
                  pl.BlockSpec((1, H_C, GRID_W, wr * GRID_W), bias_idx)],
        out_specs=pl.BlockSpec((GRID_W, W), lambda b, r: (b * rows + r, 0)),
        out_shape=jax.ShapeDtypeStruct((n, W), BF16),
        compiler_params=_cparams("parallel", "arbitrary"), name="na_attn",
    )(q, k, v, cache_k, cache_v, bias)


def _dense_attn_kernel(q_ref, k_ref, v_ref, o_ref):
    scale = D_C ** -0.5
    cols = [slice(h * D_C, (h + 1) * D_C) for h in range(H_C)]
    ss = [_dot_nt(q_ref[:, cs], k_ref[:, cs]) for cs in cols]
    heads = [[(s * scale, v_ref[:, cs])] for s, cs in zip(ss, cols)]
    for cs, o in zip(cols, _softmax_pv(heads)):
        o_ref[:, cs] = o.astype(BF16)


def _dense_attn(q, k, v, B, T):
    n = q.shape[0]
    tq = _pick_tile(T, 256)
    nq = T // tq
    W = W_HEADS
    return pl.pallas_call(
        _dense_attn_kernel, grid=(B, nq),
        in_specs=[pl.BlockSpec((tq, W), lambda b, i: (b * nq + i, 0)),
                  pl.BlockSpec((T, W), lambda b, i: (b, 0)),
                  pl.BlockSpec((T, W), lambda b, i: (b, 0))],
        out_specs=pl.BlockSpec((tq, W), lambda b, i: (b * nq + i, 0)),
        out_shape=jax.ShapeDtypeStruct((n, W), BF16),
        compiler_params=_cparams("parallel", "parallel"), name="dense_attn",
    )(q, k, v)


HALO = 8
DN_ROWS = H_B * CHUNK
INV_BASE = 8
DN_CHUNKS_PER_STEP = 2


def _bmm(a, b):
    return lax.dot_general(a, b, (((2,), (1,)), ((0,), (0,))), preferred_element_type=F32)


def _bmm_nt(a, b):
    return lax.dot_general(a, b, (((2,), (2,)), ((0,), (0,))), preferred_element_type=F32)


def _bmmb(a, b):
    return _bmm(a.astype(BF16), b.astype(BF16))


def _dn_prep_kernel(x_ref, xp_ref, xn_ref, ab_ref, abT_ref, cw_ref, alr_ref, dtr_ref, alc_ref, dtc_ref,
                    u_ref, wq_ref, kg_ref, a_ref, eg_ref, xs_ref):
    c = pl.program_id(1)
    nsteps = pl.num_programs(1)
    cps = abT_ref.shape[0]
    rows = cps * CHUNK
    pad = CONV_K // 2
    xs_ref[HALO:HALO + rows, :] = x_ref[...]
    xs_ref[0:HALO, :] = jnp.where(c > 0, xp_ref[...], 0.0)
    xs_ref[HALO + rows:, :] = jnp.where(c < nsteps - 1, xn_ref[...], 0.0)
    y = None
    for j in range(CONV_K):
        t = cw_ref[j:j + 1, :] * xs_ref[HALO - pad + j:HALO - pad + j + rows, :]
        y = t if y is None else y + t
    y = _silu(y)
    parts = []
    for j in range(cps):
        parts += _dn_chunk_terms(y[j * CHUNK:(j + 1) * CHUNK], ab_ref[j * CHUNK:(j + 1) * CHUNK, :], abT_ref[j],
                                 alr_ref, dtr_ref, alc_ref, dtc_ref, eg_ref, j)
    cat = lambda key: jnp.concatenate([p[key] for p in parts], axis=0)
    L = cat('L')
    ri = _iota((CHUNK, CHUNK), 0)
    ci = _iota((CHUNK, CHUNK), 1)
    inb = (ri // INV_BASE) == (ci // INV_BASE)
    Dg = jnp.where(inb, L, 0.0)
    X = (ri == ci).astype(F32) - Dg
    P = _bmmb(Dg, Dg)
    X = X + _bmmb(X, P)
    X = X + _bmmb(X, _bmmb(P, P))
    bs = INV_BASE
    while bs < CHUNK:
        outer = (ri // (2 * bs)) == (ci // (2 * bs))
        C = jnp.where(outer & jnp.logical_not(inb), L, 0.0)
        Xb = X.astype(BF16)
        X = X - _bmm(_bmm(Xb, C.astype(BF16)).astype(BF16), Xb)
        inb = outer
        bs *= 2
    Xb = X.astype(BF16)
    u = _bmm(Xb, cat('vb').astype(BF16))
    w = _bmm(Xb, cat('kbg').astype(BF16))
    for i, p in enumerate(parts):
        d, j = p['d'], p['j']
        rows_i = slice(i * H_B, (i + 1) * H_B)
        u_ref[d, j] = u[rows_i]
        wq_ref[d, j] = jnp.concatenate([w[rows_i], p['qg']], axis=1).astype(BF16)
        kg_ref[d, j] = p['kg'].astype(BF16)
        a_ref[d, j] = p['a_in'].astype(BF16)


def _dn_chunk_terms(y, ab, abT, alr_ref, dtr_ref, alc_ref, dtc_ref, eg_ref, j):
    W = W_HEADS

    def l2n(v):
        return v * lax.rsqrt(jnp.sum(v * v, axis=-1, keepdims=True) + EPS)

    q = jnp.stack([l2n(y[:, h * D_B:(h + 1) * D_B]) * (D_B ** -0.5) for h in range(H_B)])
    k = jnp.stack([l2n(y[:, W + h * D_B:W + (h + 1) * D_B]) for h in range(H_B)])
    v = jnp.stack([y[:, 2 * W + h * D_B:2 * W + (h + 1) * D_B] for h in range(H_B)])
    kb16 = k.astype(BF16)
    kk = _bmm_nt(kb16, kb16)
    qk = _bmm_nt(q.astype(BF16), kb16)

    nh2 = 2 * H_B
    g_col = -jnp.exp(alr_ref[...]) * _softplus(ab[:, 0:nh2] + dtr_ref[...])
    beta_col = _sigmoid(ab[:, nh2:2 * nh2])
    g_row = -jnp.exp(alc_ref[...]) * _softplus(abT[0:nh2, :] + dtc_ref[...])
    ri = _iota((CHUNK, CHUNK), 0)
    ci = _iota((CHUNK, CHUNK), 1)
    lo = (ri >= ci).astype(F32)
    up = (ri <= ci).astype(F32)
    gc = jnp.where(_iota((CHUNK, nh2), 1) < H_B, _dot_f32(lo, g_col), _dot_f32(up, g_col))
    gr = jnp.where(_iota((nh2, CHUNK), 0) < H_B, _dot_f32(g_row, up), _dot_f32(g_row, lo))
    glast = jnp.sum(g_col, axis=0, keepdims=True)
    eg_ref[j] = jnp.exp(jnp.sum(g_row, axis=1, keepdims=True)) * jnp.ones((nh2, LANES), F32)

    terms = []
    for d in range(2):
        cols = [d * H_B + h for h in range(H_B)]
        gc_h = jnp.stack([gc[:, i:i + 1] for i in cols])
        beta_h = jnp.stack([beta_col[:, i:i + 1] for i in cols])
        gl_h = jnp.stack([glast[:, i:i + 1] for i in cols])
        gr_h = jnp.stack([gr[i:i + 1, :] for i in cols])
        tri = (ri >= ci) if d == 0 else (ri <= ci)
        strict = (ri > ci) if d == 0 else (ri < ci)
        decay = jnp.where(tri, jnp.exp(jnp.where(tri, gc_h - gr_h, 0.0)), 0.0)
        egc = jnp.exp(gc_h)
        terms.append(dict(d=d, j=j,
                          L=jnp.where(strict, beta_h * kk * decay, 0.0),
                          a_in=jnp.where(tri, qk * decay, 0.0),
                          vb=v * beta_h, kbg=k * (beta_h * egc), qg=q * egc,
                          kg=k * jnp.exp(gl_h - gc_h)))
    return terms


def _dn_prep(xb, ab, conv_w, a_log, dt_bias, B, T):
    n = xb.shape[0]
    nc = T // CHUNK
    cps = _pick_tile(nc, DN_CHUNKS_PER_STEP)
    ns = nc // cps
    rows = cps * CHUNK
    nblk = n // HALO
    bps = rows // HALO
    W3 = xb.shape[1]
    nh2 = 2 * H_B
    abT3 = ab.reshape(n // CHUNK, CHUNK, 16).transpose(0, 2, 1)
    al = a_log.reshape(1, nh2)
    dt = dt_bias.reshape(1, nh2)
    st = lambda b, c: b * ns + c
    idx5 = lambda b, c: (0, st(b, c), 0, 0, 0)
    const = lambda b, c: (0, 0)
    return pl.pallas_call(
        _dn_prep_kernel, grid=(B, ns),
        in_specs=[pl.BlockSpec((rows, W3), lambda b, c: (st(b, c), 0)),
                  pl.BlockSpec((HALO, W3), lambda b, c: (jnp.maximum(st(b, c) * bps - 1, 0), 0)),
                  pl.BlockSpec((HALO, W3), lambda b, c: (jnp.minimum((st(b, c) + 1) * bps, nblk - 1), 0)),
                  pl.BlockSpec((rows, 16), lambda b, c: (st(b, c), 0)),
                  pl.BlockSpec((cps, 16, CHUNK), lambda b, c: (st(b, c), 0, 0)),
                  pl.BlockSpec(conv_w.shape, const),
                  pl.BlockSpec((1, nh2), const), pl.BlockSpec((1, nh2), const),
                  pl.BlockSpec((nh2, 1), const), pl.BlockSpec((nh2, 1), const)],
        out_specs=[pl.BlockSpec((2, cps, H_B, CHUNK, D_B), idx5),
                   pl.BlockSpec((2, cps, H_B, 2 * CHUNK, D_B), idx5),
                   pl.BlockSpec((2, cps, H_B, CHUNK, D_B), idx5),
                   pl.BlockSpec((2, cps, H_B, CHUNK, CHUNK), idx5),
                   pl.BlockSpec((cps, nh2, LANES), lambda b, c: (st(b, c), 0, 0))],
        out_shape=[jax.ShapeDtypeStruct((2, B * nc, H_B, CHUNK, D_B), F32),
                   jax.ShapeDtypeStruct((2, B * nc, H_B, 2 * CHUNK, D_B), BF16),
                   jax.ShapeDtypeStruct((2, B * nc, H_B, CHUNK, D_B), BF16),
                   jax.ShapeDtypeStruct((2, B * nc, H_B, CHUNK, CHUNK), BF16),
                   jax.ShapeDtypeStruct((B * nc, nh2, LANES), F32)],
        scratch_shapes=[pltpu.VMEM((rows + 2 * HALO, W3), F32)],
        compiler_params=_cparams("parallel", "parallel"), name="dn_prep",
    )(xb, xb, xb, ab, abT3, conv_w, al, dt, al.reshape(nh2, 1), dt.reshape(nh2, 1))


def _dn_scan_kernel(*refs):
    s0_ref = refs[0]
    per_dir = (refs[1:6], refs[6:11])
    of_ref, ob_ref, sfin_ref, s_ref = refs[11:15]
    c = pl.program_id(1)
    nc = pl.num_programs(1)

    @pl.when(c == 0)
    def _():
        s_ref[...] = s0_ref[0]

    chains = [(d, h) for d in range(2) for h in range(H_B)]
    states = [s_ref[d, h] for d, h in chains]
    rs = [_dot(per_dir[d][1][0, 0, h], s.astype(BF16)) for (d, h), s in zip(chains, states)]
    vns = [(per_dir[d][0][0, 0, h] - r[0:CHUNK]).astype(BF16) for (d, h), r in zip(chains, rs)]
    avs = [_dot(per_dir[d][3][0, 0, h], vn) for (d, h), vn in zip(chains, vns)]
    dss = [_dot_tn(per_dir[d][2][0, 0, h], vn) for (d, h), vn in zip(chains, vns)]
    for i, (d, h) in enumerate(chains):
        o_ref = of_ref if d == 0 else ob_ref
        o_ref[:, h * D_B:(h + 1) * D_B] = rs[i][CHUNK:] + avs[i]
        eg = per_dir[d][4][0, d * H_B + h:d * H_B + h + 1, :]
        s_ref[d, h] = states[i] * eg + dss[i]

    @pl.when(c == nc - 1)
    def _():
        sfin_ref[0] = s_ref[...]


def _dn_scan(prep, s0, B, T):
    u, wq, kg, a, eg = prep
    nc = T // CHUNK
    R = DN_ROWS
    nh2 = 2 * H_B

    def dir_specs(d):
        cb = (lambda b, c: b * nc + c) if d == 0 else (lambda b, c: b * nc + nc - 1 - c)
        idx5 = lambda b, c: (d, cb(b, c), 0, 0, 0)
        return [pl.BlockSpec((1, 1, H_B, CHUNK, D_B), idx5),
                pl.BlockSpec((1, 1, H_B, 2 * CHUNK, D_B), idx5),
                pl.BlockSpec((1, 1, H_B, CHUNK, D_B), idx5),
                pl.BlockSpec((1, 1, H_B, CHUNK, CHUNK), idx5),
                pl.BlockSpec((1, nh2, LANES), lambda b, c: (cb(b, c), 0, 0))]

    st = (1, 2, H_B, D_B, D_B)
    st_spec = pl.BlockSpec(st, lambda b, c: (b, 0, 0, 0, 0))
    return pl.pallas_call(
        _dn_scan_kernel, grid=(B, nc),
        in_specs=[st_spec] + dir_specs(0) + dir_specs(1),
        out_specs=[pl.BlockSpec((CHUNK, W_HEADS), lambda b, c: (b * nc + c, 0)),
                   pl.BlockSpec((CHUNK, W_HEADS), lambda b, c: (b * nc + nc - 1 - c, 0)),
                   st_spec],
        out_shape=[jax.ShapeDtypeStruct((B * T, W_HEADS), F32), jax.ShapeDtypeStruct((B * T, W_HEADS), F32),
                   jax.ShapeDtypeStruct((B,) + st[1:], F32)],
        scratch_shapes=[pltpu.VMEM(st[1:], F32)],
        compiler_params=_cparams("parallel", "arbitrary"), name="dn_scan",
    )(s0, u, wq, kg, a, eg, u, wq, kg, a, eg)


def _merge_kernel(x_ref, mod_ref, g1_ref, g2_ref, oa_ref, of_ref, ob_ref, zb_ref, gon_ref, oc_ref,
                  wg_ref, wa_ref, wb_ref, wc_ref, wo_ref, wrT_ref, x1_ref, h2_ref, lgT_ref):
    D = x_ref.shape[1]
    x = x_ref[...]
    sh1, sc1, gt1, sh2, sc2 = [mod_ref[0, j:j + 1, :] for j in range(5)]
    hb = (_rms(x, g1_ref[...]) * (1.0 + sc1) + sh1).astype(BF16)
    od = of_ref[...] + ob_ref[...]
    z = zb_ref[...]
    gon = gon_ref[...]
    obs = []
    for h in range(H_B):
        cs = slice(h * D_B, (h + 1) * D_B)
        obs.append(_rms(od[:, cs], gon) * _silu(z[:, cs]))
    ob = jnp.concatenate(obs, axis=1).astype(BF16)

    def gate(j):
        return _sigmoid(_dot(hb, wg_ref[:, j * D:(j + 1) * D]))

    merged = (gate(0) * _dot(oa_ref[...], wa_ref[...])
              + gate(1) * _dot(ob, wb_ref[...])
              + gate(2) * _dot(oc_ref[...], wc_ref[...]))
    x1 = x + gt1 * _dot(merged.astype(BF16), wo_ref[...])
    x1_ref[...] = x1
    h2 = _rms(x1, g2_ref[...]) * (1.0 + sc2) + sh2
    h2_ref[...] = h2
    E = lgT_ref.shape[0]
    h2h = h2.astype(BF16)
    h2l = (h2 - h2h.astype(F32)).astype(BF16)
    r1 = _dot_nt(wrT_ref[...], h2h)
    lgT_ref[...] = r1[:E] + (r1[E:] + _dot_nt(wrT_ref[0:E, :], h2l))


def _merge(x, mod5, g1, g2, oa, o_f, o_b, zb, g_onorm, oc, w_g, w_a, w_b, w_c, w_o, w_rT, T):
    n, D = x.shape
    tm = _pick_tile(T, 512)
    nt = T // tm
    W = W_HEADS
    E = w_rT.shape[0] // 2
    row = lambda i: (i, 0)
    const = lambda i: (0, 0)
    full = lambda a: pl.BlockSpec(a.shape, const)
    return pl.pallas_call(
        _merge_kernel, grid=(n // tm,),
        in_specs=[pl.BlockSpec((tm, D), row), pl.BlockSpec((1, 8, D), lambda i: (i // nt, 0, 0)),
                  pl.BlockSpec((1, D), const), pl.BlockSpec((1, D), const),
                  pl.BlockSpec((tm, W), row), pl.BlockSpec((tm, W), row), pl.BlockSpec((tm, W), row),
                  pl.BlockSpec((tm, W), row), pl.BlockSpec((1, D_B), const), pl.BlockSpec((tm, W), row),
                  full(w_g), full(w_a), full(w_b), full(w_c), full(w_o), full(w_rT)],
        out_specs=[pl.BlockSpec((tm, D), row), pl.BlockSpec((tm, D), row), pl.BlockSpec((E, tm), lambda i: (0, i))],
        out_shape=[jax.ShapeDtypeStruct((n, D), F32), jax.ShapeDtypeStruct((n, D), F32),
                   jax.ShapeDtypeStruct((E, n), F32)],
        compiler_params=_cparams("parallel"), name="merge",
    )(x, mod5, g1, g2, oa, o_f, o_b, zb, g_onorm, oc, w_g, w_a, w_b, w_c, w_o, w_rT)


def _route_kernel(lg_ref, idx_ref, w_ref, rc_ref, *, cap):
    E, R, _ = lg_ref.shape
    lg = lg_ref[...]
    m = jnp.max(lg, axis=0, keepdims=True)
    ex = jnp.exp(lg - m)
    aff = ex / jnp.sum(ex, axis=0, keepdims=True)
    bits = pltpu.bitcast(aff, I32)

    def count(mask):
        t = jnp.sum(jnp.where(mask, 1.0, 0.0), axis=2, keepdims=True)
        return jnp.sum(t, axis=1, keepdims=True)

    def bisect(_, carry):
        lo, hi = carry
        mid = lo + lax.shift_right_logical(hi - lo, 1)
        ge = count(bits >= mid) >= cap
        return jnp.where(ge, mid, lo), jnp.where(ge, hi, mid)

    lo0 = jnp.zeros((E, 1, 1), I32)
    hi0 = jnp.full((E, 1, 1), 0x7F800000, I32)
    thr, _ = lax.fori_loop(0, 31, bisect, (lo0, hi0))
    gt = bits > thr
    eq = bits == thr
    need = cap - count(gt)

    lane_r = _iota((LANES, LANES), 0)
    lane_c = _iota((LANES, LANES), 1)
    u_lane = (lane_r <= lane_c).astype(BF16)
    row_r = _iota((R, R), 0)
    row_c = _iota((R, R), 1)
    sl_row = (row_c < row_r).astype(BF16)
    u_row = (row_r <= row_c).astype(BF16)
    ones8 = jnp.ones((8, LANES), BF16)
    eye_l = (lane_r == lane_c).astype(BF16)

    def prefix(masks):
        ws = [_dot(mk.astype(BF16), u_lane) for mk in masks]
        tots = [(jnp.zeros((R, LANES), F32) + w[:, LANES - 1:LANES]).astype(BF16) for w in ws]
        return [w + _dot(sl_row, tot) for w, tot in zip(ws, tots)], ws

    gtf = jnp.where(gt, 1.0, 0.0)
    eqf = jnp.where(eq, 1.0, 0.0)
    experts = range(E)
    pes = prefix([eqf[e] for e in experts])[0]
    sels = [jnp.maximum(gtf[e], jnp.where(pes[e] <= need[e], eqf[e], 0.0)) for e in experts]
    ws = prefix(sels)[1]
    rts = [_dot_nt(ones8, s.astype(BF16)) for s in sels]
    rcs = [_dot(rt.astype(BF16), u_row) for rt in rts]
    for e in experts:
        w_ref[e] = ws[e]
        rc_ref[e] = rcs[e]

    nblk = cap // LANES
    G = _pick_tile(nblk, ROUTE_BLOCKS_PER_STEP)
    ngrp = nblk // G
    s_base = _iota((G * LANES, 1), 0).astype(F32)

    def slot_group(i, _):
        e = i // ngrp
        jg = i % ngrp
        w_e = w_ref[e]
        rc_row = rc_ref[e][0:1, :]
        s_col = s_base + jnp.asarray(jg * (G * LANES), F32)
        below = rc_row <= s_col
        rho = jnp.sum(jnp.where(below, 1.0, 0.0), axis=1, keepdims=True)
        before = jnp.max(jnp.where(below, rc_row, 0.0), axis=1, keepdims=True)
        onehot = (_iota((G * LANES, R), 1).astype(F32) == rho).astype(BF16)
        wg = _dot(onehot, w_e.astype(BF16))
        lam = jnp.sum(jnp.where(wg <= s_col - before, 1.0, 0.0), axis=1, keepdims=True)
        blocks = [slice(g * LANES, (g + 1) * LANES) for g in range(G)]
        rho_l = [_dot_nt(ones8, (eye_l * rho[b]).astype(BF16)) for b in blocks]
        lam_l = [_dot_nt(ones8, (eye_l * lam[b]).astype(BF16)) for b in blocks]
        for g in range(G):
            idx_ref[i * G + g] = (rho_l[g][0:1, :] * LANES + lam_l[g][0:1, :]).astype(I32)
        return 0

    lax.fori_loop(0, E * ngrp, slot_group, 0)


def _route(logits_t, cap):
    E, n = logits_t.shape
    R = n // LANES
    nblk = cap // LANES
    idx = pl.pallas_call(
        functools.partial(_route_kernel, cap=cap),
        out_shape=jax.ShapeDtypeStruct((E * nblk, 1, LANES), I32),
        scratch_shapes=[pltpu.VMEM((E, R, LANES), F32), pltpu.VMEM((E, 8, R), F32)],
        compiler_params=pltpu.CompilerParams(vmem_limit_bytes=VMEM_LIMIT_BYTES), name="route",
    )(logits_t.reshape(E, R, LANES))
    return idx.reshape(E, cap)


def _moe_kernel(idx_ref, idxn_ref, h_hbm, wr_ref, wg_ref, wu_ref, wd_ref, acc_in, acc_hbm,
                xbuf, mbuf, sem_x, sem_m, sem_s, *, nt):
    del acc_in
    e = pl.program_id(0)
    k = pl.program_id(1)
    total = pl.num_programs(0) * nt
    step = e * nt + k
    slot = step % 2
    tm = xbuf.shape[1]

    def row_copies(src_of, dst_of, sem, ids):
        def start(r, _):
            t = ids[0, 0, r]
            pltpu.make_async_copy(src_of(t, r), dst_of(t, r), sem).start()
            return 0
        lax.fori_loop(0, tm, start, 0, unroll=8)

    def wait_rows(sem, buf):
        pltpu.make_async_copy(buf, buf, sem).wait()

    def gather_x(ids, s):
        row_copies(lambda t, r: h_hbm.at[pl.ds(t, 1)], lambda t, r: xbuf.at[s, pl.ds(r, 1)], sem_x.at[s], ids)

    @pl.when(step == 0)
    def _():
        gather_x(idx_ref, 0)

    wait_rows(sem_x.at[slot], xbuf.at[slot])

    @pl.when(step > 0)
    def _():
        wait_rows(sem_s, mbuf)

    xe = xbuf[slot].astype(BF16)
    lg = _dot(xe, wr_ref[...])
    lane = _iota(lg.shape, 1)
    lg = jnp.where(lane < N_EXPERTS, lg, NEG)
    ex = jnp.exp(lg - jnp.max(lg, axis=-1, keepdims=True))
    aff = ex / jnp.sum(ex, axis=-1, keepdims=True)
    gate = jnp.sum(jnp.where(lane == e, aff, 0.0), axis=-1, keepdims=True)

    F = wg_ref.shape[2]
    fc = F // MOE_ISSUE_CHUNKS
    ngroups = 4 * MOE_ISSUE_CHUNKS
    rows_per = tm // ngroups
    issued = [0]

    def issue_group():
        for r in range(issued[0], issued[0] + rows_per):
            pltpu.make_async_copy(h_hbm.at[pl.ds(idxn_ref[0, 0, r], 1)], xbuf.at[1 - slot, pl.ds(r, 1)],
                                  sem_x.at[1 - slot]).start()
            pltpu.make_async_copy(acc_hbm.at[pl.ds(idx_ref[0, 0, r], 1)], mbuf.at[pl.ds(r, 1)], sem_m).start()
        issued[0] += rows_per

    ye = None
    for c in range(MOE_ISSUE_CHUNKS):
        cols = slice(c * fc, (c + 1) * fc)
        issue_group()
        g = _dot(xe, wg_ref[0, :, cols])
        issue_group()
        u = _dot(xe, wu_ref[0, :, cols])
        issue_group()
        hid = (_silu(g) * u).astype(BF16)
        issue_group()
        part = _dot(hid, wd_ref[0, cols, :])
        ye = part if ye is None else ye + part

    wait_rows(sem_m, mbuf)
    mbuf[...] = mbuf[...] + ye * gate
    row_copies(lambda t, r: mbuf.at[pl.ds(r, 1)], lambda t, r: acc_hbm.at[pl.ds(t, 1)], sem_s, idx_ref)

    @pl.when(step == total - 1)
    def _():
        wait_rows(sem_s, mbuf)
        wait_rows(sem_x.at[1 - slot], xbuf.at[1 - slot])


def _moe(h2, idx, w_router_pad, w_gate, w_up, w_down, acc):
    n, D = h2.shape
    E, cap = idx.shape
    F = w_gate.shape[-1]
    tm = _pick_tile(cap, 256)
    nt = cap // tm
    idx2 = idx.reshape(E * nt, 1, tm)
    last = E * nt - 1
    smem_blk = lambda f: pl.BlockSpec((1, 1, tm), f, memory_space=pltpu.SMEM)
    return pl.pallas_call(
        functools.partial(_moe_kernel, nt=nt), grid=(E, nt),
        in_specs=[smem_blk(lambda e, k: (e * nt + k, 0, 0)),
                  smem_blk(lambda e, k: (jnp.minimum(e * nt + k + 1, last), 0, 0)),
                  pl.BlockSpec(memory_space=pl.ANY),
                  pl.BlockSpec(w_router_pad.shape, lambda e, k: (0, 0)),
                  pl.BlockSpec((1, D, F), lambda e, k: (e, 0, 0)),
                  pl.BlockSpec((1, D, F), lambda e, k: (e, 0, 0)),
                  pl.BlockSpec((1, F, D), lambda e, k: (e, 0, 0)),
                  pl.BlockSpec(memory_space=pl.ANY)],
        out_specs=pl.BlockSpec(memory_space=pl.ANY),
        out_shape=jax.ShapeDtypeStruct((n, D), F32),
        input_output_aliases={7: 0},
        scratch_shapes=[pltpu.VMEM((2, tm, D), F32), pltpu.VMEM((tm, D), F32),
                        pltpu.SemaphoreType.DMA((2,)), pltpu.SemaphoreType.DMA(()), pltpu.SemaphoreType.DMA(())],
        compiler_params=_cparams("arbitrary", "arbitrary"), name="moe",
    )(idx2, idx2, h2, w_router_pad, w_gate, w_up, w_down, acc)


def _resid_kernel(x_ref, m_ref, gt_ref, g_ref, o_ref, *, final):
    x2 = x_ref[...] + gt_ref[0] * m_ref[...]
    o_ref[...] = _rms(x2, g_ref[...]) if final else x2


def _resid(x1, m, gt2, g_final, T, final):
    n, D = x1.shape
    tm = _pick_tile(T, 512)
    nt = T // tm
    row = lambda i: (i, 0)
    return pl.pallas_call(
        functools.partial(_resid_kernel, final=final), grid=(n // tm,),
        in_specs=[pl.BlockSpec((tm, D), row), pl.BlockSpec((tm, D), row),
                  pl.BlockSpec((1, 1, D), lambda i: (i // nt, 0, 0)), pl.BlockSpec((1, D), lambda i: (0, 0))],
        out_specs=pl.BlockSpec((tm, D), row), out_shape=jax.ShapeDtypeStruct((n, D), F32),
        compiler_params=_cparams("parallel"), name="resid",
    )(x1, m, gt2, g_final)


def _lambda_init(layer):
    return 0.8 - 0.6 * math.exp(-0.3 * layer)


def _hi_lo_rows(w):
    hi = w.astype(BF16)
    return jnp.concatenate([hi, (w - hi.astype(F32)).astype(BF16)], axis=0)


def _prep_weights(p, l):
    W = W_HEADS
    D = p['w_in'].shape[1]
    w_in = p['w_in'][l]
    nab = 4 * H_B
    o_ab = 3 * W + 4 * W
    o_c = o_ab + nab
    o_g = o_c + 3 * W
    w_main = jnp.concatenate([w_in[:, :o_ab], w_in[:, o_c:o_g]], axis=1).astype(BF16)
    w_ab = w_in[:, o_ab:o_c].astype(BF16)
    E = p['w_router'].shape[-1]
    w_r = p['w_router'][l]
    return dict(
        w_main=w_main, w_ab=w_ab,
        w_g=w_in[:, o_g:].astype(BF16),
        w_a=p['w_br_a'][l].astype(BF16), w_b=p['w_br_b'][l].astype(BF16), w_c=p['w_br_c'][l].astype(BF16),
        w_o=p['w_out'][l].astype(BF16),
        w_rT=_hi_lo_rows(w_r.T),
        w_r_pad=jnp.pad(w_r, ((0, 0), (0, LANES - E))).astype(BF16),
        w_gate=p['w_e_gate'][l].astype(BF16), w_up=p['w_e_up'][l].astype(BF16), w_down=p['w_e_down'][l].astype(BF16),
        g1=p['g_norm1'][l][None], g2=p['g_norm2'][l][None],
        conv_w=p['conv_w'][l], a_log=p['a_log'][l], dt_bias=p['dt_bias'][l],
        g_onorm=p['g_onorm'][l][None], lam_qk=p['lam_qk'][l], g_subln=p['g_subln'][l][None], rpb=p['rpb'][l],
    )


def _trunk_layer(x, mod, lw, l, B, T, g_final, final, ctx=None):
    n, D = x.shape
    lam_init = _lambda_init(l)
    latent = ctx is not None
    sh1, sc1 = mod[:, 0:1], mod[:, 1:2]
    outs = _inproj(x, sc1, sh1, lw['g1'], lw['w_main'], lw['w_ab'], T,
                   _rope_tables(T) if latent else None, emit_kv=not latent)
    qa, ka, va, xb, zb, qc, kc, vc, ab = outs[:9]
    prep = _dn_prep(xb, ab, lw['conv_w'], lw['a_log'], lw['dt_bias'], B, T)
    if latent:
        ck_a, cv_a, ck_c, cv_c, c_state = ctx
        oa = _diff_attn(qa, ka, va, ck_a, cv_a, lw['lam_qk'], lw['g_subln'], B, T, lam_init)
        o_f, o_b, _ = _dn_scan(prep, c_state, B, T)
        oc = _na_attn(qc, kc, vc, ck_c, cv_c, lw['rpb'], B, T)
        new_ctx = None
    else:
        oa = _diff_attn(qa, ka, va, None, None, lw['lam_qk'], lw['g_subln'], B, T, lam_init)
        s0 = jnp.zeros((B, 2, H_B, D_B, D_B), F32)
        o_f, o_b, s_fin = _dn_scan(prep, s0, B, T)
        oc = _dense_attn(qc, kc, vc, B, T)
        new_ctx = tuple(a.reshape(B, T, 4, 128) for a in outs[9:13]) + (s_fin,)
    mod5 = jnp.pad(mod[:, :5], ((0, 0), (0, 3), (0, 0)))
    x1, h2, lg_t = _merge(x, mod5, lw['g1'], lw['g2'], oa, o_f, o_b, zb, lw['g_onorm'], oc,
                          lw['w_g'], lw['w_a'], lw['w_b'], lw['w_c'], lw['w_o'], lw['w_rT'], T)
    cap = EC_FACTOR * n // N_EXPERTS
    idx = _route(lg_t, cap)
    m = _moe(h2, idx, lw['w_r_pad'], lw['w_gate'], lw['w_up'], lw['w_down'], jnp.zeros((n, D), F32))
    x2 = _resid(x1, m, mod[:, 5:6], g_final, T, final)
    return x2, new_ctx


def kernel(x_prompt, x_sample, c, cache_diff_k, cache_diff_v, cache_na_k, cache_na_v, state_delta, c_ctx, w_ada, b_ada, g_norm1, g_norm2, w_in, conv_w, a_log, dt_bias, g_onorm, lam_qk, g_subln, rpb, w_br_a, w_br_b, w_br_c, w_out, w_router, w_e_gate, w_e_up, w_e_down, g_final):
    p = dict(w_in=w_in, conv_w=conv_w, a_log=a_log, dt_bias=dt_bias, g_onorm=g_onorm, lam_qk=lam_qk,
             g_subln=g_subln, rpb=rpb, w_br_a=w_br_a, w_br_b=w_br_b, w_br_c=w_br_c, w_out=w_out,
             w_router=w_router, w_e_gate=w_e_gate, w_e_up=w_e_up, w_e_down=w_e_down,
             g_norm1=g_norm1, g_norm2=g_norm2)
    depth = w_in.shape[0]
    Bp, Tp, D = x_prompt.shape
    Bs, Ts, _ = x_sample.shape
    gf = g_final[None]

    nrow = 1 + Bs
    cond = jnp.pad(jnp.concatenate([c_ctx[None], c], axis=0), ((0, -nrow % 8), (0, 0)))
    mod = _adaln(cond, w_ada, b_ada).reshape(depth, -1, 6, D)
    layers = [_prep_weights(p, l) for l in range(depth)]

    xp = x_prompt.reshape(Bp * Tp, D)
    new_ctx = []
    for l in range(depth):
        mod_ctx = jnp.broadcast_to(mod[l, 0:1], (Bp, 6, D))
        xp, nc = _trunk_layer(xp, mod_ctx, layers[l], l, Bp, Tp, gf, l == depth - 1)
        new_ctx.append(nc)

    xs = x_sample.reshape(Bs * Ts, D)
    for l in range(depth):
        Lc = cache_diff_k.shape[2]
        flat = lambda a: a[:, l].reshape(Bs * Lc, W_HEADS).astype(BF16)
        ctx = (flat(cache_diff_k), flat(cache_diff_v), flat(cache_na_k), flat(cache_na_v), state_delta[:, l])
        xs, _ = _trunk_layer(xs, mod[l, 1:1 + Bs], layers[l], l, Bs, Ts, gf, l == depth - 1, ctx)

    stack = lambda j: jnp.stack([nc[j] for nc in new_ctx], axis=1)
    return (xp.reshape(Bp, Tp, D), xs.reshape(Bs, Ts, D), stack(0), stack(1), stack(2), stack(3), stack(4))
```

```python
import functools
import math

import jax
import jax.numpy as jnp
from jax import lax
from jax.experimental import pallas as pl
from jax.experimental.pallas import tpu as pltpu

F32 = jnp.float32
BF16 = jnp.bfloat16
I32 = jnp.int32

H_A, D_A = 4, 64
H_B, D_B = 4, 128
H_C, D_C = 4, 128
CONV_K = 5
CHUNK = 64
GRID_W = 64
WIN_R, WIN_C = 8, 16
N_EXPERTS = 16
EC_FACTOR = 2
ROPE_BASE = 10000.0
EPS = 1e-6
NEG = -1e30
W_HEADS = 512
ATTN_KEY_CHUNK = 1024
LOG2E = 1.4426950408889634
ROUTE_BLOCKS_PER_STEP = 4
NA_ROWS_PER_STEP = 4
MOE_SLOT_TILE = 256

LANES = 128
VMEM_LIMIT_BYTES = 56 * 1024 * 1024


def _cparams(*sem):
    return pltpu.CompilerParams(dimension_semantics=sem, vmem_limit_bytes=VMEM_LIMIT_BYTES)


def _dot(a, b):
    return jnp.dot(a, b, preferred_element_type=F32)


def _dot_nt(a, b):
    return lax.dot_general(a, b, (((1,), (1,)), ((), ())), preferred_element_type=F32)


def _dot_tn(a, b):
    return lax.dot_general(a, b, (((0,), (0,)), ((), ())), preferred_element_type=F32)


def _dot_f32(a, b):
    return jnp.dot(a, b, preferred_element_type=F32, precision=lax.Precision.HIGHEST)


def _sigmoid(x):
    return 1.0 / (1.0 + jnp.exp(-x))


def _silu(x):
    return x * _sigmoid(x)


def _softplus(x):
    return jnp.maximum(x, 0.0) + jnp.log(1.0 + jnp.exp(-jnp.abs(x)))


def _rms(x, g):
    return x * lax.rsqrt(jnp.mean(x * x, axis=-1, keepdims=True) + EPS) * g


def _iota(shape, dim):
    return lax.broadcasted_iota(I32, shape, dim)


def _pick_tile(n, pref):
    t = min(n, pref)
    while n % t:
        t //= 2
    return t


def _adaln_kernel(c_ref, w_ref, b_ref, o_ref):
    c = c_ref[...]
    s = _silu(c).astype(BF16)
    o_ref[0] = _dot(s, w_ref[0].astype(BF16)) + b_ref[0]


def _adaln(cond, w_ada, b_ada):
    L, D, N = w_ada.shape
    R = cond.shape[0]
    tn = _pick_tile(N, 1536)
    return pl.pallas_call(
        _adaln_kernel,
        grid=(L, N // tn),
        in_specs=[pl.BlockSpec((R, D), lambda l, j: (0, 0)),
                  pl.BlockSpec((1, D, tn), lambda l, j: (l, 0, j)),
                  pl.BlockSpec((1, 1, tn), lambda l, j: (l, 0, j))],
        out_specs=pl.BlockSpec((1, R, tn), lambda l, j: (l, 0, j)),
        out_shape=jax.ShapeDtypeStruct((L, R, N), F32),
        compiler_params=_cparams("parallel", "parallel"),
        name="adaln",
    )(cond, w_ada, b_ada.reshape(L, 1, N))


def _inproj_kernel(*refs, rope, emit_kv):
    it = iter(refs)
    x_ref, sc_ref, sh_ref, g_ref, w_ref, wab_ref = [next(it) for _ in range(6)]
    if rope:
        cos_ref, sa_ref, sb_ref = [next(it) for _ in range(3)]
    qa_ref, ka_ref, va_ref, xb_ref, zb_ref, qc_ref, kc_ref, vc_ref, ab_ref = [next(it) for _ in range(9)]
    if emit_kv:
        ka32_ref, va32_ref, kc32_ref, vc32_ref = [next(it) for _ in range(4)]

    x = x_ref[...]
    h = _rms(x, g_ref[...]) * (1.0 + sc_ref[0]) + sh_ref[0]
    hb = h.astype(BF16)
    W = W_HEADS

    def proj(j, n=1):
        return _dot(hb, w_ref[:, j * W:(j + n) * W])

    def roped(v):
        if not rope:
            return v
        n = v.shape[-1]
        return v * cos_ref[...] + pltpu.roll(v, n - D_A // 4, 1) * sa_ref[...] + pltpu.roll(v, D_A // 4, 1) * sb_ref[...]

    qa = proj(0)
    qa_ref[...] = (roped(qa) * (D_A ** -0.5 * LOG2E)).astype(BF16)
    ka = proj(1)
    ka_ref[...] = roped(ka).astype(BF16)
    va = proj(2)
    va_ref[...] = va.astype(BF16)
    xb_ref[...] = proj(3, 3)
    zb_ref[...] = proj(6)
    qc_ref[...] = proj(7).astype(BF16)
    kc = proj(8)
    kc_ref[...] = kc.astype(BF16)
    vc = proj(9)
    vc_ref[...] = vc.astype(BF16)
    ab_ref[...] = _dot(hb, wab_ref[...])
    if emit_kv:
        ka32_ref[...] = ka
        va32_ref[...] = va
        kc32_ref[...] = kc
        vc32_ref[...] = vc


def _inproj(x, sc, sh, g, w_main, w_ab, T, rope_tabs, emit_kv):
    n, D = x.shape
    tm = _pick_tile(T, 512)
    nt = T // tm
    W = W_HEADS
    rope = rope_tabs is not None
    row = lambda i: (i, 0)
    per_b = lambda i: (i // nt, 0, 0)
    const = lambda i: (0, 0)
    in_specs = [pl.BlockSpec((tm, D), row),
                pl.BlockSpec((1, 1, D), per_b), pl.BlockSpec((1, 1, D), per_b),
                pl.BlockSpec((1, D), const),
                pl.BlockSpec(w_main.shape, const), pl.BlockSpec(w_ab.shape, const)]
    args = [x, sc, sh, g, w_main, w_ab]
    if rope:
        in_specs += [pl.BlockSpec((tm, W), lambda i: (i % nt, 0))] * 3
        args += list(rope_tabs)
    bf = lambda: jax.ShapeDtypeStruct((n, W), BF16)
    f32 = lambda w: jax.ShapeDtypeStruct((n, w), F32)
    out_shape = [bf(), bf(), bf(), f32(3 * W), f32(W), bf(), bf(), bf(), f32(16)]
    out_specs = [pl.BlockSpec((tm, W), row)] * 3 + [pl.BlockSpec((tm, 3 * W), row), pl.BlockSpec((tm, W), row)] \
        + [pl.BlockSpec((tm, W), row)] * 3 + [pl.BlockSpec((tm, 16), row)]
    if emit_kv:
        out_shape += [f32(W)] * 4
        out_specs += [pl.BlockSpec((tm, W), row)] * 4
    return pl.pallas_call(
        functools.partial(_inproj_kernel, rope=rope, emit_kv=emit_kv),
        grid=(n // tm,), in_specs=in_specs, out_specs=out_specs, out_shape=out_shape,
        compiler_params=_cparams("parallel"), name="inproj",
    )(*args)


def _rope_tables(T):
    quarter = D_A // 4
    inv = 1.0 / (ROPE_BASE ** (jnp.arange(quarter, dtype=F32) / quarter))
    t = jnp.arange(T)
    pos = jnp.stack([t // GRID_W, t % GRID_W], axis=-1).astype(F32)
    ang = pos[:, :, None] * inv
    cos, sin = jnp.cos(ang), jnp.sin(ang)
    cos64 = jnp.concatenate([cos[:, 0], cos[:, 0], cos[:, 1], cos[:, 1]], axis=-1)
    zero = jnp.zeros_like(sin[:, 0])
    sa64 = jnp.concatenate([-sin[:, 0], zero, -sin[:, 1], zero], axis=-1)
    sb64 = jnp.concatenate([zero, sin[:, 0], zero, sin[:, 1]], axis=-1)
    rep = W_HEADS // D_A
    return tuple(jnp.tile(a, (1, rep)) for a in (cos64, sa64, sb64))


def _diff_attn_kernel(*refs, has_cache, lam_init):
    if has_cache:
        q_ref, k_ref, v_ref, kc_ref, vc_ref, lq_ref, g_ref, o_ref = refs
    else:
        q_ref, k_ref, v_ref, lq_ref, g_ref, o_ref = refs
    q = q_ref[...]
    lane = _iota(q.shape, 1)
    zero = jnp.zeros_like(q)
    qs = (jnp.where(lane < D_A, q, zero), jnp.where(lane >= D_A, q, zero))
    lq = lq_ref[...]
    lam = (jnp.exp(jnp.sum(lq[0:1] * lq[1:2], axis=-1, keepdims=True))
           - jnp.exp(jnp.sum(lq[2:3] * lq[3:4], axis=-1, keepdims=True)) + lam_init)
    hd = q.shape[1]

    def with_ones(v):
        one = jnp.where(_iota(v.shape, 1) == 0, 1.0, 0.0).astype(BF16)
        return jnp.concatenate([v, one], axis=1)

    T = k_ref.shape[0]
    tk = _pick_tile(T, ATTN_KEY_CHUNK)
    chunks = [(k_ref[j * tk:(j + 1) * tk, :], with_ones(v_ref[j * tk:(j + 1) * tk, :])) for j in range(T // tk)]
    if has_cache:
        chunks.append((kc_ref[...], with_ones(vc_ref[...])))
    tq = q.shape[0]
    outs = []
    for comp in range(2):
        m = jnp.full((tq, 1), NEG, F32)
        acc = jnp.zeros((tq, 2 * hd), F32)
        for kj, vxj in chunks:
            s = _dot_nt(qs[comp], kj)
            m_new = jnp.maximum(m, jnp.max(s, axis=-1, keepdims=True))
            acc = acc * jnp.exp2(m - m_new) + _dot(jnp.exp2(s - m_new).astype(BF16), vxj)
            m = m_new
        outs.append(acc[:, :hd] * (1.0 / acc[:, hd:hd + 1]))
    o = outs[0] - lam * outs[1]
    o_ref[...] = (_rms(o, g_ref[...]) * (1.0 - lam_init)).astype(BF16)


def _diff_attn(q, k, v, cache_k, cache_v, lam_qk, g_sub, B, T, lam_init):
    n = q.shape[0]
    tq = _pick_tile(T, 512)
    nq = T // tq
    has_cache = cache_k is not None
    hd = 2 * D_A
    in_specs = [pl.BlockSpec((tq, hd), lambda b, h, i: (b * nq + i, h)),
                pl.BlockSpec((T, hd), lambda b, h, i: (b, h)),
                pl.BlockSpec((T, hd), lambda b, h, i: (b, h))]
    args = [q, k, v]
    if has_cache:
        Lc = cache_k.shape[0] // B
        in_specs += [pl.BlockSpec((Lc, hd), lambda b, h, i: (b, h))] * 2
        args += [cache_k, cache_v]
    in_specs += [pl.BlockSpec(lam_qk.shape, lambda b, h, i: (0, 0)), pl.BlockSpec((1, hd), lambda b, h, i: (0, 0))]
    args += [lam_qk, g_sub]
    return pl.pallas_call(
        functools.partial(_diff_attn_kernel, has_cache=has_cache, lam_init=lam_init),
        grid=(B, H_A, nq), in_specs=in_specs,
        out_specs=pl.BlockSpec((tq, hd), lambda b, h, i: (b * nq + i, h)),
        out_shape=jax.ShapeDtypeStruct((n, W_HEADS), BF16),
        compiler_params=_cparams("parallel", "parallel", "parallel"), name="diff_attn",
    )(*args)


def _softmax_pv(heads):
    def fold(xs, op):
        out = xs[0]
        for x in xs[1:]:
            out = op(out, x)
        return out

    ms = [fold([jnp.max(s, axis=-1, keepdims=True) for s, _ in parts], jnp.maximum) for parts in heads]
    ps = [[jnp.exp(s - m) for s, _ in parts] for parts, m in zip(heads, ms)]
    rs = [1.0 / fold([jnp.sum(p, axis=-1, keepdims=True) for p in pp], jnp.add) for pp in ps]
    pbs = [[(p * r).astype(BF16) for p in pp] for pp, r in zip(ps, rs)]
    return [fold([_dot(p, v) for p, (_, v) in zip(pp, parts)], jnp.add) for pp, parts in zip(pbs, heads)]


def _na_kernel(*refs, rows, wr, rps):
    q_ref, k_ref, v_ref, kc_ref, vc_ref = refs[:5]
    bias_refs = refs[5:5 + rps]
    o_ref = refs[5 + rps]
    nloc = wr * GRID_W
    scale = D_C ** -0.5
    cols = [slice(h * D_C, (h + 1) * D_C) for h in range(H_C)]
    chains = []
    for i in range(rps):
        r = pl.program_id(1) * rps + i
        rs = jnp.clip(r - wr // 2, 0, rows - wr)
        start = pl.multiple_of(rs * GRID_W, GRID_W)
        chains += [(i, start, h) for h in range(H_C)]
    qrow = lambda i: slice(i * GRID_W, (i + 1) * GRID_W)
    qs = [q_ref[qrow(i), cols[h]] for i, _, h in chains]
    s_loc = [_dot_nt(q, k_ref[pl.ds(start, nloc), cols[h]]) for q, (_, start, h) in zip(qs, chains)]
    s_ctx = [_dot_nt(q, kc_ref[:, cols[h]]) for q, (_, _, h) in zip(qs, chains)]
    heads = [[(sl * scale + bias_refs[i][0, h], v_ref[pl.ds(start, nloc), cols[h]]),
              (sc * scale, vc_ref[:, cols[h]])] for sl, sc, (i, start, h) in zip(s_loc, s_ctx, chains)]
    for (i, _, h), o in zip(chains, _softmax_pv(heads)):
        o_ref[qrow(i), cols[h]] = o.astype(BF16)


def _na_bias_table(rpb, rows, wr):
    col = jnp.arange(GRID_W)
    cs = jnp.clip(col - WIN_C // 2, 0, GRID_W - WIN_C)
    col_mask = (col[None, :] >= cs[:, None]) & (col[None, :] < cs[:, None] + WIN_C)
    dc_idx = jnp.clip(col[None, :] - col[:, None] + WIN_C - 1, 0, 2 * WIN_C - 2)
    onehot = (dc_idx[:, :, None] == jnp.arange(2 * WIN_C - 1)).astype(F32)
    base = jnp.einsum('hdj,qkj->hdqk', rpb.astype(F32), onehot, precision=lax.Precision.HIGHEST)
    base = jnp.where(col_mask[None, None], base, NEG)
    tabs = []
    for off in range(wr):
        lo = WIN_R - 1 - off
        tabs.append(base[:, lo:lo + wr].transpose(0, 2, 1, 3).reshape(H_C, GRID_W, wr * GRID_W))
    return jnp.stack(tabs)


def _na_attn(q, k, v, cache_k, cache_v, rpb, B, T):
    n = q.shape[0]
    rows = T // GRID_W
    wr = min(WIN_R, rows)
    Lc = cache_k.shape[0] // B
    bias = _na_bias_table(rpb, rows, wr)
    W = W_HEADS

    rps = _pick_tile(rows, NA_ROWS_PER_STEP)
    nsteps = rows // rps

    def bias_spec(i):
        def idx(b, s):
            r = s * rps + i
            return (r - jnp.clip(r - wr // 2, 0, rows - wr), 0, 0, 0)
        return pl.BlockSpec((1, H_C, GRID_W, wr * GRID_W), idx)

    return pl.pallas_call(
        functools.partial(_na_kernel, rows=rows, wr=wr, rps=rps),
        grid=(B, nsteps),
        in_specs=[pl.BlockSpec((rps * GRID_W, W), lambda b, s: (b * nsteps + s, 0)),
                  pl.BlockSpec((T, W), lambda b, s: (b, 0)),
                  pl.BlockSpec((T, W), lambda b, s: (b, 0)),
                  pl.BlockSpec((Lc, W), lambda b, s: (b, 0)),
                  pl.BlockSpec((Lc, W), lambda b, s: (b, 0))] + [bias_spec(i) for i in range(rps)],
        out_specs=pl.BlockSpec((rps * GRID_W, W), lambda b, s: (b * nsteps + s, 0)),
        out_shape=jax.ShapeDtypeStruct((n, W), BF16),
        compiler_params=_cparams("parallel", "arbitrary"), name="na_attn",
    )(q, k, v, cache_k, cache_v, *([bias] * rps))


def _dense_attn_kernel(q_ref, k_ref, v_ref, o_ref):
    scale = D_C ** -0.5
    cols = [slice(h * D_C, (h + 1) * D_C) for h in range(H_C)]
    ss = [_dot_nt(q_ref[:, cs], k_ref[:, cs]) for cs in cols]
    heads = [[(s * scale, v_ref[:, cs])] for s, cs in zip(ss, cols)]
    for cs, o in zip(cols, _softmax_pv(heads)):
        o_ref[:, cs] = o.astype(BF16)


def _dense_attn(q, k, v, B, T):
    n = q.shape[0]
    tq = _pick_tile(T, 256)
    nq = T // tq
    W = W_HEADS
    return pl.pallas_call(
        _dense_attn_kernel, grid=(B, nq),
        in_specs=[pl.BlockSpec((tq, W), lambda b, i: (b * nq + i, 0)),
                  pl.BlockSpec((T, W), lambda b, i: (b, 0)),
                  pl.BlockSpec((T, W), lambda b, i: (b, 0))],
        out_specs=pl.BlockSpec((tq, W), lambda b, i: (b * nq + i, 0)),
        out_shape=jax.ShapeDtypeStruct((n, W), BF16),
        compiler_params=_cparams("parallel", "parallel"), name="dense_attn",
    )(q, k, v)


HALO = 8
DN_ROWS = H_B * CHUNK
INV_BASE = 8
DN_CHUNKS_PER_STEP = 2


def _bmm(a, b):
    return lax.dot_general(a, b, (((2,), (1,)), ((0,), (0,))), preferred_element_type=F32)


def _bmm_nt(a, b):
    return lax.dot_general(a, b, (((2,), (2,)), ((0,), (0,))), preferred_element_type=F32)


def _bmmb(a, b):
    return _bmm(a.astype(BF16), b.astype(BF16))


def _dn_prep_kernel(x_ref, xp_ref, xn_ref, ab_ref, abT_ref, cw_ref, alr_ref, dtr_ref, alc_ref, dtc_ref,
                    u_ref, wq_ref, kg_ref, a_ref, eg_ref, xs_ref):
    c = pl.program_id(1)
    nsteps = pl.num_programs(1)
    cps = abT_ref.shape[0]
    rows = cps * CHUNK
    pad = CONV_K // 2
    xs_ref[HALO:HALO + rows, :] = x_ref[...]
    xs_ref[0:HALO, :] = jnp.where(c > 0, xp_ref[...], 0.0)
    xs_ref[HALO + rows:, :] = jnp.where(c < nsteps - 1, xn_ref[...], 0.0)
    y = None
    for j in range(CONV_K):
        t = cw_ref[j:j + 1, :] * xs_ref[HALO - pad + j:HALO - pad + j + rows, :]
        y = t if y is None else y + t
    y = _silu(y)
    parts = []
    for j in range(cps):
        parts += _dn_chunk_terms(y[j * CHUNK:(j + 1) * CHUNK], ab_ref[j * CHUNK:(j + 1) * CHUNK, :], abT_ref[j],
                                 alr_ref, dtr_ref, alc_ref, dtc_ref, eg_ref, j)
    cat = lambda key: jnp.concatenate([p[key] for p in parts], axis=0)
    L = cat('L')
    ri = _iota((CHUNK, CHUNK), 0)
    ci = _iota((CHUNK, CHUNK), 1)
    inb = (ri // INV_BASE) == (ci // INV_BASE)
    Dg = jnp.where(inb, L, 0.0)
    X = (ri == ci).astype(F32) - Dg
    P = _bmmb(Dg, Dg)
    X = X + _bmmb(X, P)
    X = X + _bmmb(X, _bmmb(P, P))
    bs = INV_BASE
    while bs < CHUNK:
        outer = (ri // (2 * bs)) == (ci // (2 * bs))
        C = jnp.where(outer & jnp.logical_not(inb), L, 0.0)
        Xb = X.astype(BF16)
        X = X - _bmm(_bmm(Xb, C.astype(BF16)).astype(BF16), Xb)
        inb = outer
        bs *= 2
    Xb = X.astype(BF16)
    u = _bmm(Xb, cat('vb').astype(BF16))
    w = _bmm(Xb, cat('kbg').astype(BF16))
    for i, p in enumerate(parts):
        d, j = p['d'], p['j']
        rows_i = slice(i * H_B, (i + 1) * H_B)
        u_ref[d, j] = u[rows_i]
        wq_ref[d, j] = jnp.concatenate([w[rows_i], p['qg']], axis=1).astype(BF16)
        kg_ref[d, j] = p['kg'].astype(BF16)
        a_ref[d, j] = p['a_in'].astype(BF16)


def _dn_chunk_terms(y, ab, abT, alr_ref, dtr_ref, alc_ref, dtc_ref, eg_ref, j):
    W = W_HEADS

    def l2n(v):
        return v * lax.rsqrt(jnp.sum(v * v, axis=-1, keepdims=True) + EPS)

    q = jnp.stack([l2n(y[:, h * D_B:(h + 1) * D_B]) * (D_B ** -0.5) for h in range(H_B)])
    k = jnp.stack([l2n(y[:, W + h * D_B:W + (h + 1) * D_B]) for h in range(H_B)])
    v = jnp.stack([y[:, 2 * W + h * D_B:2 * W + (h + 1) * D_B] for h in range(H_B)])
    kb16 = k.astype(BF16)
    kk = _bmm_nt(kb16, kb16)
    qk = _bmm_nt(q.astype(BF16), kb16)

    nh2 = 2 * H_B
    g_col = -jnp.exp(alr_ref[...]) * _softplus(ab[:, 0:nh2] + dtr_ref[...])
    beta_col = _sigmoid(ab[:, nh2:2 * nh2])
    g_row = -jnp.exp(alc_ref[...]) * _softplus(abT[0:nh2, :] + dtc_ref[...])
    ri = _iota((CHUNK, CHUNK), 0)
    ci = _iota((CHUNK, CHUNK), 1)
    lo = (ri >= ci).astype(F32)
    up = (ri <= ci).astype(F32)
    gc = jnp.where(_iota((CHUNK, nh2), 1) < H_B, _dot_f32(lo, g_col), _dot_f32(up, g_col))
    gr = jnp.where(_iota((nh2, CHUNK), 0) < H_B, _dot_f32(g_row, up), _dot_f32(g_row, lo))
    glast = jnp.sum(g_col, axis=0, keepdims=True)
    eg_ref[j] = jnp.exp(jnp.sum(g_row, axis=1, keepdims=True)) * jnp.ones((nh2, LANES), F32)

    terms = []
    for d in range(2):
        cols = [d * H_B + h for h in range(H_B)]
        gc_h = jnp.stack([gc[:, i:i + 1] for i in cols])
        beta_h = jnp.stack([beta_col[:, i:i + 1] for i in cols])
        gl_h = jnp.stack([glast[:, i:i + 1] for i in cols])
        gr_h = jnp.stack([gr[i:i + 1, :] for i in cols])
        tri = (ri >= ci) if d == 0 else (ri <= ci)
        strict = (ri > ci) if d == 0 else (ri < ci)
        decay = jnp.where(tri, jnp.exp(jnp.where(tri, gc_h - gr_h, 0.0)), 0.0)
        egc = jnp.exp(gc_h)
        terms.append(dict(d=d, j=j,
                          L=jnp.where(strict, beta_h * kk * decay, 0.0),
                          a_in=jnp.where(tri, qk * decay, 0.0),
                          vb=v * beta_h, kbg=k * (beta_h * egc), qg=q * egc,
                          kg=k * jnp.exp(gl_h - gc_h)))
    return terms


def _dn_prep(xb, ab, conv_w, a_log, dt_bias, B, T):
    n = xb.shape[0]
    nc = T // CHUNK
    cps = _pick_tile(nc, DN_CHUNKS_PER_STEP)
    ns = nc // cps
    rows = cps * CHUNK
    nblk = n // HALO
    bps = rows // HALO
    W3 = xb.shape[1]
    nh2 = 2 * H_B
    abT3 = ab.reshape(n // CHUNK, CHUNK, 16).transpose(0, 2, 1)
    al = a_log.reshape(1, nh2)
    dt = dt_bias.reshape(1, nh2)
    st = lambda b, c: b * ns + c
    idx5 = lambda b, c: (0, st(b, c), 0, 0, 0)
    const = lambda b, c: (0, 0)
    return pl.pallas_call(
        _dn_prep_kernel, grid=(B, ns),
        in_specs=[pl.BlockSpec((rows, W3), lambda b, c: (st(b, c), 0)),
                  pl.BlockSpec((HALO, W3), lambda b, c: (jnp.maximum(st(b, c) * bps - 1, 0), 0)),
                  pl.BlockSpec((HALO, W3), lambda b, c: (jnp.minimum((st(b, c) + 1) * bps, nblk - 1), 0)),
                  pl.BlockSpec((rows, 16), lambda b, c: (st(b, c), 0)),
                  pl.BlockSpec((cps, 16, CHUNK), lambda b, c: (st(b, c), 0, 0)),
                  pl.BlockSpec(conv_w.shape, const),
                  pl.BlockSpec((1, nh2), const), pl.BlockSpec((1, nh2), const),
                  pl.BlockSpec((nh2, 1), const), pl.BlockSpec((nh2, 1), const)],
        out_specs=[pl.BlockSpec((2, cps, H_B, CHUNK, D_B), idx5),
                   pl.BlockSpec((2, cps, H_B, 2 * CHUNK, D_B), idx5),
                   pl.BlockSpec((2, cps, H_B, CHUNK, D_B), idx5),
                   pl.BlockSpec((2, cps, H_B, CHUNK, CHUNK), idx5),
                   pl.BlockSpec((cps, nh2, LANES), lambda b, c: (st(b, c), 0, 0))],
        out_shape=[jax.ShapeDtypeStruct((2, B * nc, H_B, CHUNK, D_B), F32),
                   jax.ShapeDtypeStruct((2, B * nc, H_B, 2 * CHUNK, D_B), BF16),
                   jax.ShapeDtypeStruct((2, B * nc, H_B, CHUNK, D_B), BF16),
                   jax.ShapeDtypeStruct((2, B * nc, H_B, CHUNK, CHUNK), BF16),
                   jax.ShapeDtypeStruct((B * nc, nh2, LANES), F32)],
        scratch_shapes=[pltpu.VMEM((rows + 2 * HALO, W3), F32)],
        compiler_params=_cparams("parallel", "parallel"), name="dn_prep",
    )(xb, xb, xb, ab, abT3, conv_w, al, dt, al.reshape(nh2, 1), dt.reshape(nh2, 1))


def _dn_scan_kernel(*refs):
    s0_ref = refs[0]
    per_dir = (refs[1:6], refs[6:11])
    of_ref, ob_ref, sfin_ref, s_ref = refs[11:15]
    c = pl.program_id(1)
    nc = pl.num_programs(1)

    @pl.when(c == 0)
    def _():
        s_ref[...] = s0_ref[0]

    cps = per_dir[0][4].shape[0]
    chains = [(d, h) for d in range(2) for h in range(H_B)]
    states = [s_ref[d, h] for d, h in chains]
    for sub in range(cps):
        js = [sub if d == 0 else cps - 1 - sub for d, _ in chains]
        rs = [_dot(per_dir[d][1][0, j, h], s.astype(BF16)) for (d, h), j, s in zip(chains, js, states)]
        vns = [(per_dir[d][0][0, j, h] - r[0:CHUNK]).astype(BF16) for (d, h), j, r in zip(chains, js, rs)]
        avs = [_dot(per_dir[d][3][0, j, h], vn) for (d, h), j, vn in zip(chains, js, vns)]
        dss = [_dot_tn(per_dir[d][2][0, j, h], vn) for (d, h), j, vn in zip(chains, js, vns)]
        new_states = []
        for i, ((d, h), j) in enumerate(zip(chains, js)):
            o_ref = of_ref if d == 0 else ob_ref
            o_ref[j * CHUNK:(j + 1) * CHUNK, h * D_B:(h + 1) * D_B] = rs[i][CHUNK:] + avs[i]
            eg = per_dir[d][4][j, d * H_B + h:d * H_B + h + 1, :]
            new_states.append(states[i] * eg + dss[i])
        states = new_states
    for (d, h), s in zip(chains, states):
        s_ref[d, h] = s

    @pl.when(c == nc - 1)
    def _():
        sfin_ref[0] = s_ref[...]


def _dn_scan(prep, s0, B, T):
    u, wq, kg, a, eg = prep
    nc = T // CHUNK
    R = DN_ROWS
    nh2 = 2 * H_B

    cps = _pick_tile(nc, DN_CHUNKS_PER_STEP)
    ns = nc // cps

    def dir_specs(d):
        cb = (lambda b, c: b * ns + c) if d == 0 else (lambda b, c: b * ns + ns - 1 - c)
        idx5 = lambda b, c: (d, cb(b, c), 0, 0, 0)
        return [pl.BlockSpec((1, cps, H_B, CHUNK, D_B), idx5),
                pl.BlockSpec((1, cps, H_B, 2 * CHUNK, D_B), idx5),
                pl.BlockSpec((1, cps, H_B, CHUNK, D_B), idx5),
                pl.BlockSpec((1, cps, H_B, CHUNK, CHUNK), idx5),
                pl.BlockSpec((cps, nh2, LANES), lambda b, c: (cb(b, c), 0, 0))]

    st = (1, 2, H_B, D_B, D_B)
    st_spec = pl.BlockSpec(st, lambda b, c: (b, 0, 0, 0, 0))
    return pl.pallas_call(
        _dn_scan_kernel, grid=(B, ns),
        in_specs=[st_spec] + dir_specs(0) + dir_specs(1),
        out_specs=[pl.BlockSpec((cps * CHUNK, W_HEADS), lambda b, c: (b * ns + c, 0)),
                   pl.BlockSpec((cps * CHUNK, W_HEADS), lambda b, c: (b * ns + ns - 1 - c, 0)),
                   st_spec],
        out_shape=[jax.ShapeDtypeStruct((B * T, W_HEADS), F32), jax.ShapeDtypeStruct((B * T, W_HEADS), F32),
                   jax.ShapeDtypeStruct((B,) + st[1:], F32)],
        scratch_shapes=[pltpu.VMEM(st[1:], F32)],
        compiler_params=_cparams("parallel", "arbitrary"), name="dn_scan",
    )(s0, u, wq, kg, a, eg, u, wq, kg, a, eg)


def _merge_kernel(x_ref, mod_ref, g1_ref, g2_ref, oa_ref, of_ref, ob_ref, zb_ref, gon_ref, oc_ref,
                  wg_ref, wa_ref, wb_ref, wc_ref, wo_ref, wrT_ref, x1_ref, h2_ref, lgT_ref):
    D = x_ref.shape[1]
    x = x_ref[...]
    sh1, sc1, gt1, sh2, sc2 = [mod_ref[0, j:j + 1, :] for j in range(5)]
    hb = (_rms(x, g1_ref[...]) * (1.0 + sc1) + sh1).astype(BF16)
    od = of_ref[...] + ob_ref[...]
    z = zb_ref[...]
    gon = gon_ref[...]
    obs = []
    for h in range(H_B):
        cs = slice(h * D_B, (h + 1) * D_B)
        obs.append(_rms(od[:, cs], gon) * _silu(z[:, cs]))
    ob = jnp.concatenate(obs, axis=1).astype(BF16)

    def gate(j):
        return _sigmoid(_dot(hb, wg_ref[:, j * D:(j + 1) * D]))

    merged = (gate(0) * _dot(oa_ref[...], wa_ref[...])
              + gate(1) * _dot(ob, wb_ref[...])
              + gate(2) * _dot(oc_ref[...], wc_ref[...]))
    x1 = x + gt1 * _dot(merged.astype(BF16), wo_ref[...])
    x1_ref[...] = x1
    h2 = _rms(x1, g2_ref[...]) * (1.0 + sc2) + sh2
    h2_ref[...] = h2
    E = lgT_ref.shape[0]
    h2h = h2.astype(BF16)
    h2l = (h2 - h2h.astype(F32)).astype(BF16)
    r1 = _dot_nt(wrT_ref[...], h2h)
    lgT_ref[...] = r1[:E] + (r1[E:] + _dot_nt(wrT_ref[0:E, :], h2l))


def _merge(x, mod5, g1, g2, oa, o_f, o_b, zb, g_onorm, oc, w_g, w_a, w_b, w_c, w_o, w_rT, T):
    n, D = x.shape
    tm = _pick_tile(T, 512)
    nt = T // tm
    W = W_HEADS
    E = w_rT.shape[0] // 2
    row = lambda i: (i, 0)
    const = lambda i: (0, 0)
    full = lambda a: pl.BlockSpec(a.shape, const)
    return pl.pallas_call(
        _merge_kernel, grid=(n // tm,),
        in_specs=[pl.BlockSpec((tm, D), row), pl.BlockSpec((1, 8, D), lambda i: (i // nt, 0, 0)),
                  pl.BlockSpec((1, D), const), pl.BlockSpec((1, D), const),
                  pl.BlockSpec((tm, W), row), pl.BlockSpec((tm, W), row), pl.BlockSpec((tm, W), row),
                  pl.BlockSpec((tm, W), row), pl.BlockSpec((1, D_B), const), pl.BlockSpec((tm, W), row),
                  full(w_g), full(w_a), full(w_b), full(w_c), full(w_o), full(w_rT)],
        out_specs=[pl.BlockSpec((tm, D), row), pl.BlockSpec((tm, D), row), pl.BlockSpec((E, tm), lambda i: (0, i))],
        out_shape=[jax.ShapeDtypeStruct((n, D), F32), jax.ShapeDtypeStruct((n, D), F32),
                   jax.ShapeDtypeStruct((E, n), F32)],
        compiler_params=_cparams("parallel"), name="merge",
    )(x, mod5, g1, g2, oa, o_f, o_b, zb, g_onorm, oc, w_g, w_a, w_b, w_c, w_o, w_rT)


def _route_kernel(lg_ref, idx_ref, w_ref, rc_ref, *, cap):
    E, R, _ = lg_ref.shape
    lg = lg_ref[...]
    m = jnp.max(lg, axis=0, keepdims=True)
    ex = jnp.exp(lg - m)
    aff = ex / jnp.sum(ex, axis=0, keepdims=True)
    bits = pltpu.bitcast(aff, I32)

    def count(mask):
        t = jnp.sum(jnp.where(mask, 1.0, 0.0), axis=2, keepdims=True)
        return jnp.sum(t, axis=1, keepdims=True)

    def bisect(_, carry):
        lo, hi = carry
        mid = lo + lax.shift_right_logical(hi - lo, 1)
        ge = count(bits >= mid) >= cap
        return jnp.where(ge, mid, lo), jnp.where(ge, hi, mid)

    lo0 = jnp.zeros((E, 1, 1), I32)
    hi0 = jnp.full((E, 1, 1), 0x7F800000, I32)
    thr, _ = lax.fori_loop(0, 31, bisect, (lo0, hi0))
    gt = bits > thr
    eq = bits == thr
    need = cap - count(gt)

    lane_r = _iota((LANES, LANES), 0)
    lane_c = _iota((LANES, LANES), 1)
    u_lane = (lane_r <= lane_c).astype(BF16)
    row_r = _iota((R, R), 0)
    row_c = _iota((R, R), 1)
    sl_row = (row_c < row_r).astype(BF16)
    u_row = (row_r <= row_c).astype(BF16)
    ones8 = jnp.ones((8, LANES), BF16)
    eye_l = (lane_r == lane_c).astype(BF16)

    def prefix(masks):
        ws = [_dot(mk.astype(BF16), u_lane) for mk in masks]
        tots = [(jnp.zeros((R, LANES), F32) + w[:, LANES - 1:LANES]).astype(BF16) for w in ws]
        return [w + _dot(sl_row, tot) for w, tot in zip(ws, tots)], ws

    gtf = jnp.where(gt, 1.0, 0.0)
    eqf = jnp.where(eq, 1.0, 0.0)
    experts = range(E)
    pes = prefix([eqf[e] for e in experts])[0]
    sels = [jnp.maximum(gtf[e], jnp.where(pes[e] <= need[e], eqf[e], 0.0)) for e in experts]
    ws = prefix(sels)[1]
    rts = [_dot_nt(ones8, s.astype(BF16)) for s in sels]
    rcs = [_dot(rt.astype(BF16), u_row) for rt in rts]
    for e in experts:
        w_ref[e] = ws[e]
        rc_ref[e] = rcs[e]

    nblk = cap // LANES
    G = _pick_tile(nblk, ROUTE_BLOCKS_PER_STEP)
    ngrp = nblk // G
    s_base = _iota((G * LANES, 1), 0).astype(F32)

    def slot_group(i, _):
        e = i // ngrp
        jg = i % ngrp
        w_e = w_ref[e]
        rc_row = rc_ref[e][0:1, :]
        s_col = s_base + jnp.asarray(jg * (G * LANES), F32)
        below = rc_row <= s_col
        rho = jnp.sum(jnp.where(below, 1.0, 0.0), axis=1, keepdims=True)
        before = jnp.max(jnp.where(below, rc_row, 0.0), axis=1, keepdims=True)
        onehot = (_iota((G * LANES, R), 1).astype(F32) == rho).astype(BF16)
        wg = _dot(onehot, w_e.astype(BF16))
        lam = jnp.sum(jnp.where(wg <= s_col - before, 1.0, 0.0), axis=1, keepdims=True)
        blocks = [slice(g * LANES, (g + 1) * LANES) for g in range(G)]
        rho_l = [_dot_nt(ones8, (eye_l * rho[b]).astype(BF16)) for b in blocks]
        lam_l = [_dot_nt(ones8, (eye_l * lam[b]).astype(BF16)) for b in blocks]
        for g in range(G):
            idx_ref[i * G + g] = (rho_l[g][0:1, :] * LANES + lam_l[g][0:1, :]).astype(I32)
        return 0

    lax.fori_loop(0, E * ngrp, slot_group, 0)


def _route(logits_t, cap):
    E, n = logits_t.shape
    R = n // LANES
    nblk = cap // LANES
    idx = pl.pallas_call(
        functools.partial(_route_kernel, cap=cap),
        out_shape=jax.ShapeDtypeStruct((E * nblk, 1, LANES), I32),
        scratch_shapes=[pltpu.VMEM((E, R, LANES), F32), pltpu.VMEM((E, 8, R), F32)],
        compiler_params=pltpu.CompilerParams(vmem_limit_bytes=VMEM_LIMIT_BYTES), name="route",
    )(logits_t.reshape(E, R, LANES))
    return idx.reshape(E, cap)


def _moe_kernel(idx_ref, idxn_ref, h_hbm, wr_ref, wg_ref, wu_ref, wd_ref, acc_in, acc_hbm,
                xbuf, mbuf, sem_x, sem_m, sem_s, *, nt):
    del acc_in
    e = pl.program_id(0)
    k = pl.program_id(1)
    total = pl.num_programs(0) * nt
    step = e * nt + k
    slot = step % 2
    tm = xbuf.shape[1]

    def row_copies(src_of, dst_of, sem, ids):
        def start(r, _):
            t = ids[0, 0, r]
            pltpu.make_async_copy(src_of(t, r), dst_of(t, r), sem).start()
            return 0
        lax.fori_loop(0, tm, start, 0, unroll=8)

    def wait_rows(sem, buf):
        pltpu.make_async_copy(buf, buf, sem).wait()

    def gather_x(ids, s):
        row_copies(lambda t, r: h_hbm.at[pl.ds(t, 1)], lambda t, r: xbuf.at[s, pl.ds(r, 1)], sem_x.at[s], ids)

    @pl.when(step == 0)
    def _():
        gather_x(idx_ref, 0)

    wait_rows(sem_x.at[slot], xbuf.at[slot])

    @pl.when(step + 1 < total)
    def _():
        gather_x(idxn_ref, 1 - slot)

    @pl.when(step > 0)
    def _():
        wait_rows(sem_s, mbuf)
    row_copies(lambda t, r: acc_hbm.at[pl.ds(t, 1)], lambda t, r: mbuf.at[pl.ds(r, 1)], sem_m, idx_ref)

    xe = xbuf[slot].astype(BF16)
    hid = _silu(_dot(xe, wg_ref[0])) * _dot(xe, wu_ref[0])
    lg = _dot(xe, wr_ref[...])
    lane = _iota(lg.shape, 1)
    lg = jnp.where(lane < N_EXPERTS, lg, NEG)
    ex = jnp.exp(lg - jnp.max(lg, axis=-1, keepdims=True))
    aff = ex / jnp.sum(ex, axis=-1, keepdims=True)
    gate = jnp.sum(jnp.where(lane == e, aff, 0.0), axis=-1, keepdims=True)
    ye = _dot(hid.astype(BF16), wd_ref[0]) * gate

    wait_rows(sem_m, mbuf)
    mbuf[...] = mbuf[...] + ye
    row_copies(lambda t, r: mbuf.at[pl.ds(r, 1)], lambda t, r: acc_hbm.at[pl.ds(t, 1)], sem_s, idx_ref)

    @pl.when(step == total - 1)
    def _():
        wait_rows(sem_s, mbuf)


def _moe(h2, idx, w_router_pad, w_gate, w_up, w_down, acc):
    n, D = h2.shape
    E, cap = idx.shape
    F = w_gate.shape[-1]
    tm = _pick_tile(cap, MOE_SLOT_TILE)
    nt = cap // tm
    idx2 = idx.reshape(E * nt, 1, tm)
    last = E * nt - 1
    smem_blk = lambda f: pl.BlockSpec((1, 1, tm), f, memory_space=pltpu.SMEM)
    return pl.pallas_call(
        functools.partial(_moe_kernel, nt=nt), grid=(E, nt),
        in_specs=[smem_blk(lambda e, k: (e * nt + k, 0, 0)),
                  smem_blk(lambda e, k: (jnp.minimum(e * nt + k + 1, last), 0, 0)),
                  pl.BlockSpec(memory_space=pl.ANY),
                  pl.BlockSpec(w_router_pad.shape, lambda e, k: (0, 0)),
                  pl.BlockSpec((1, D, F), lambda e, k: (e, 0, 0)),
                  pl.BlockSpec((1, D, F), lambda e, k: (e, 0, 0)),
                  pl.BlockSpec((1, F, D), lambda e, k: (e, 0, 0)),
                  pl.BlockSpec(memory_space=pl.ANY)],
        out_specs=pl.BlockSpec(memory_space=pl.ANY),
        out_shape=jax.ShapeDtypeStruct((n, D), F32),
        input_output_aliases={7: 0},
        scratch_shapes=[pltpu.VMEM((2, tm, D), F32), pltpu.VMEM((tm, D), F32),
                        pltpu.SemaphoreType.DMA((2,)), pltpu.SemaphoreType.DMA(()), pltpu.SemaphoreType.DMA(())],
        compiler_params=_cparams("arbitrary", "arbitrary"), name="moe",
    )(idx2, idx2, h2, w_router_pad, w_gate, w_up, w_down, acc)


def _resid_kernel(x_ref, m_ref, gt_ref, g_ref, o_ref, *, final):
    x2 = x_ref[...] + gt_ref[0] * m_ref[...]
    o_ref[...] = _rms(x2, g_ref[...]) if final else x2


def _resid(x1, m, gt2, g_final, T, final):
    n, D = x1.shape
    tm = _pick_tile(T, 512)
    nt = T // tm
    row = lambda i: (i, 0)
    return pl.pallas_call(
        functools.partial(_resid_kernel, final=final), grid=(n // tm,),
        in_specs=[pl.BlockSpec((tm, D), row), pl.BlockSpec((tm, D), row),
                  pl.BlockSpec((1, 1, D), lambda i: (i // nt, 0, 0)), pl.BlockSpec((1, D), lambda i: (0, 0))],
        out_specs=pl.BlockSpec((tm, D), row), out_shape=jax.ShapeDtypeStruct((n, D), F32),
        compiler_params=_cparams("parallel"), name="resid",
    )(x1, m, gt2, g_final)


def _lambda_init(layer):
    return 0.8 - 0.6 * math.exp(-0.3 * layer)


def _hi_lo_rows(w):
    hi = w.astype(BF16)
    return jnp.concatenate([hi, (w - hi.astype(F32)).astype(BF16)], axis=0)


def _prep_weights(p, l):
    W = W_HEADS
    D = p['w_in'].shape[1]
    w_in = p['w_in'][l]
    nab = 4 * H_B
    o_ab = 3 * W + 4 * W
    o_c = o_ab + nab
    o_g = o_c + 3 * W
    w_main = jnp.concatenate([w_in[:, :o_ab], w_in[:, o_c:o_g]], axis=1).astype(BF16)
    w_ab = w_in[:, o_ab:o_c].astype(BF16)
    E = p['w_router'].shape[-1]
    w_r = p['w_router'][l]
    return dict(
        w_main=w_main, w_ab=w_ab,
        w_g=w_in[:, o_g:].astype(BF16),
        w_a=p['w_br_a'][l].astype(BF16), w_b=p['w_br_b'][l].astype(BF16), w_c=p['w_br_c'][l].astype(BF16),
        w_o=p['w_out'][l].astype(BF16),
        w_rT=_hi_lo_rows(w_r.T),
        w_r_pad=jnp.pad(w_r, ((0, 0), (0, LANES - E))).astype(BF16),
        w_gate=p['w_e_gate'][l].astype(BF16), w_up=p['w_e_up'][l].astype(BF16), w_down=p['w_e_down'][l].astype(BF16),
        g1=p['g_norm1'][l][None], g2=p['g_norm2'][l][None],
        conv_w=p['conv_w'][l], a_log=p['a_log'][l], dt_bias=p['dt_bias'][l],
        g_onorm=p['g_onorm'][l][None], lam_qk=p['lam_qk'][l], g_subln=p['g_subln'][l][None], rpb=p['rpb'][l],
    )


def _trunk_layer(x, mod, lw, l, B, T, g_final, final, ctx=None):
    n, D = x.shape
    lam_init = _lambda_init(l)
    latent = ctx is not None
    sh1, sc1 = mod[:, 0:1], mod[:, 1:2]
    outs = _inproj(x, sc1, sh1, lw['g1'], lw['w_main'], lw['w_ab'], T,
                   _rope_tables(T) if latent else None, emit_kv=not latent)
    qa, ka, va, xb, zb, qc, kc, vc, ab = outs[:9]
    prep = _dn_prep(xb, ab, lw['conv_w'], lw['a_log'], lw['dt_bias'], B, T)
    if latent:
        ck_a, cv_a, ck_c, cv_c, c_state = ctx
        oa = _diff_attn(qa, ka, va, ck_a, cv_a, lw['lam_qk'], lw['g_subln'], B, T, lam_init)
        o_f, o_b, _ = _dn_scan(prep, c_state, B, T)
        oc = _na_attn(qc, kc, vc, ck_c, cv_c, lw['rpb'], B, T)
        new_ctx = None
    else:
        oa = _diff_attn(qa, ka, va, None, None, lw['lam_qk'], lw['g_subln'], B, T, lam_init)
        s0 = jnp.zeros((B, 2, H_B, D_B, D_B), F32)
        o_f, o_b, s_fin = _dn_scan(prep, s0, B, T)
        oc = _dense_attn(qc, kc, vc, B, T)
        new_ctx = tuple(a.reshape(B, T, 4, 128) for a in outs[9:13]) + (s_fin,)
    mod5 = jnp.pad(mod[:, :5], ((0, 0), (0, 3), (0, 0)))
    x1, h2, lg_t = _merge(x, mod5, lw['g1'], lw['g2'], oa, o_f, o_b, zb, lw['g_onorm'], oc,
                          lw['w_g'], lw['w_a'], lw['w_b'], lw['w_c'], lw['w_o'], lw['w_rT'], T)
    cap = EC_FACTOR * n // N_EXPERTS
    idx = _route(lg_t, cap)
    m = _moe(h2, idx, lw['w_r_pad'], lw['w_gate'], lw['w_up'], lw['w_down'], jnp.zeros((n, D), F32))
    x2 = _resid(x1, m, mod[:, 5:6], g_final, T, final)
    return x2, new_ctx


def kernel(x_prompt, x_sample, c, cache_diff_k, cache_diff_v, cache_na_k, cache_na_v, state_delta, c_ctx, w_ada, b_ada, g_norm1, g_norm2, w_in, conv_w, a_log, dt_bias, g_onorm, lam_qk, g_subln, rpb, w_br_a, w_br_b, w_br_c, w_out, w_router, w_e_gate, w_e_up, w_e_down, g_final):
    p = dict(w_in=w_in, conv_w=conv_w, a_log=a_log, dt_bias=dt_bias, g_onorm=g_onorm, lam_qk=lam_qk,
             g_subln=g_subln, rpb=rpb, w_br_a=w_br_a, w_br_b=w_br_b, w_br_c=w_br_c, w_out=w_out,
             w_router=w_router, w_e_gate=w_e_gate, w_e_up=w_e_up, w_e_down=w_e_down,
             g_norm1=g_norm1, g_norm2=g_norm2)
    depth = w_in.shape[0]
    Bp, Tp, D = x_prompt.shape
    Bs, Ts, _ = x_sample.shape
    gf = g_final[None]

    nrow = 1 + Bs
    cond = jnp.pad(jnp.concatenate([c_ctx[None], c], axis=0), ((0, -nrow % 8), (0, 0)))
    mod = _adaln(cond, w_ada, b_ada).reshape(depth, -1, 6, D)
    layers = [_prep_weights(p, l) for l in range(depth)]

    xp = x_prompt.reshape(Bp * Tp, D)
    new_ctx = []
    for l in range(depth):
        mod_ctx = jnp.broadcast_to(mod[l, 0:1], (Bp, 6, D))
        xp, nc = _trunk_layer(xp, mod_ctx, layers[l], l, Bp, Tp, gf, l == depth - 1)
        new_ctx.append(nc)

    xs = x_sample.reshape(Bs * Ts, D)
    for l in range(depth):
        Lc = cache_diff_k.shape[2]
        flat = lambda a: a[:, l].reshape(Bs * Lc, W_HEADS).astype(BF16)
        ctx = (flat(cache_diff_k), flat(cache_diff_v), flat(cache_na_k), flat(cache_na_v), state_delta[:, l])
        xs, _ = _trunk_layer(xs, mod[l, 1:1 + Bs], layers[l], l, Bs, Ts, gf, l == depth - 1, ctx)

    stack = lambda j: jnp.stack([nc[j] for nc in new_ctx], axis=1)
    return (xp.reshape(Bp, Tp, D), xs.reshape(Bs, Ts, D), stack(0), stack(1), stack(2), stack(3), stack(4))
```

```python
import functools
import math

import jax
import jax.numpy as jnp
from jax import lax
from jax.experimental import pallas as pl
from jax.experimental.pallas import tpu as pltpu

F32 = jnp.float32
BF16 = jnp.bfloat16
I32 = jnp.int32

H_A, D_A = 4, 64
H_B, D_B = 4, 128
H_C, D_C = 4, 128
CONV_K = 5
CHUNK = 64
GRID_W = 64
WIN_R, WIN_C = 8, 16
N_EXPERTS = 16
EC_FACTOR = 2
ROPE_BASE = 10000.0
EPS = 1e-6
NEG = -1e30
W_HEADS = 512
ATTN_KEY_CHUNK = 1024
LOG2E = 1.4426950408889634
ROUTE_BLOCKS_PER_STEP = 4
NA_ROWS_PER_STEP = 4
ROW_COPY_UNROLL = 8
MOE_SLOT_TILE = 256

LANES = 128
VMEM_LIMIT_BYTES = 56 * 1024 * 1024


def _cparams(*sem):
    return pltpu.CompilerParams(dimension_semantics=sem, vmem_limit_bytes=VMEM_LIMIT_BYTES)


def _dot(a, b):
    return jnp.dot(a, b, preferred_element_type=F32)


def _dot_nt(a, b):
    return lax.dot_general(a, b, (((1,), (1,)), ((), ())), preferred_element_type=F32)


def _dot_tn(a, b):
    return lax.dot_general(a, b, (((0,), (0,)), ((), ())), preferred_element_type=F32)


def _dot_f32(a, b):
    return jnp.dot(a, b, preferred_element_type=F32, precision=lax.Precision.HIGHEST)


def _sigmoid(x):
    return 1.0 / (1.0 + jnp.exp(-x))


def _silu(x):
    return x * _sigmoid(x)


def _softplus(x):
    return jnp.maximum(x, 0.0) + jnp.log(1.0 + jnp.exp(-jnp.abs(x)))


def _rms(x, g):
    return x * lax.rsqrt(jnp.mean(x * x, axis=-1, keepdims=True) + EPS) * g


def _iota(shape, dim):
    return lax.broadcasted_iota(I32, shape, dim)


def _pick_tile(n, pref):
    t = min(n, pref)
    while n % t:
        t //= 2
    return t


def _adaln_kernel(c_ref, w_ref, b_ref, o_ref):
    c = c_ref[...]
    s = _silu(c).astype(BF16)
    o_ref[0] = _dot(s, w_ref[0].astype(BF16)) + b_ref[0]


def _adaln(cond, w_ada, b_ada):
    L, D, N = w_ada.shape
    R = cond.shape[0]
    tn = _pick_tile(N, 1536)
    return pl.pallas_call(
        _adaln_kernel,
        grid=(L, N // tn),
        in_specs=[pl.BlockSpec((R, D), lambda l, j: (0, 0)),
                  pl.BlockSpec((1, D, tn), lambda l, j: (l, 0, j)),
                  pl.BlockSpec((1, 1, tn), lambda l, j: (l, 0, j))],
        out_specs=pl.BlockSpec((1, R, tn), lambda l, j: (l, 0, j)),
        out_shape=jax.ShapeDtypeStruct((L, R, N), F32),
        compiler_params=_cparams("parallel", "parallel"),
        name="adaln",
    )(cond, w_ada, b_ada.reshape(L, 1, N))


def _inproj_kernel(*refs, rope, emit_kv):
    it = iter(refs)
    x_ref, sc_ref, sh_ref, g_ref, w_ref, wab_ref = [next(it) for _ in range(6)]
    if rope:
        cos_ref, sa_ref, sb_ref = [next(it) for _ in range(3)]
    qa_ref, ka_ref, va_ref, xb_ref, zb_ref, qc_ref, kc_ref, vc_ref, ab_ref = [next(it) for _ in range(9)]
    if emit_kv:
        ka32_ref, va32_ref, kc32_ref, vc32_ref = [next(it) for _ in range(4)]

    x = x_ref[...]
    h = _rms(x, g_ref[...]) * (1.0 + sc_ref[0]) + sh_ref[0]
    hb = h.astype(BF16)
    W = W_HEADS

    def proj(j, n=1):
        return _dot(hb, w_ref[:, j * W:(j + n) * W])

    def roped(v):
        if not rope:
            return v
        n = v.shape[-1]
        return v * cos_ref[...] + pltpu.roll(v, n - D_A // 4, 1) * sa_ref[...] + pltpu.roll(v, D_A // 4, 1) * sb_ref[...]

    qa = proj(0)
    qa_ref[...] = (roped(qa) * (D_A ** -0.5 * LOG2E)).astype(BF16)
    ka = proj(1)
    ka_ref[...] = roped(ka).astype(BF16)
    va = proj(2)
    va_ref[...] = va.astype(BF16)
    xb_ref[...] = proj(3, 3)
    zb_ref[...] = proj(6)
    qc_ref[...] = proj(7).astype(BF16)
    kc = proj(8)
    kc_ref[...] = kc.astype(BF16)
    vc = proj(9)
    vc_ref[...] = vc.astype(BF16)
    ab_ref[...] = _dot(hb, wab_ref[...])
    if emit_kv:
        ka32_ref[...] = ka
        va32_ref[...] = va
        kc32_ref[...] = kc
        vc32_ref[...] = vc


def _inproj(x, sc, sh, g, w_main, w_ab, T, rope_tabs, emit_kv):
    n, D = x.shape
    tm = _pick_tile(T, 512)
    nt = T // tm
    W = W_HEADS
    rope = rope_tabs is not None
    row = lambda i: (i, 0)
    per_b = lambda i: (i // nt, 0, 0)
    const = lambda i: (0, 0)
    in_specs = [pl.BlockSpec((tm, D), row),
                pl.BlockSpec((1, 1, D), per_b), pl.BlockSpec((1, 1, D), per_b),
                pl.BlockSpec((1, D), const),
                pl.BlockSpec(w_main.shape, const), pl.BlockSpec(w_ab.shape, const)]
    args = [x, sc, sh, g, w_main, w_ab]
    if rope:
        in_specs += [pl.BlockSpec((tm, W), lambda i: (i % nt, 0))] * 3
        args += list(rope_tabs)
    bf = lambda: jax.ShapeDtypeStruct((n, W), BF16)
    f32 = lambda w: jax.ShapeDtypeStruct((n, w), F32)
    out_shape = [bf(), bf(), bf(), f32(3 * W), f32(W), bf(), bf(), bf(), f32(16)]
    out_specs = [pl.BlockSpec((tm, W), row)] * 3 + [pl.BlockSpec((tm, 3 * W), row), pl.BlockSpec((tm, W), row)] \
        + [pl.BlockSpec((tm, W), row)] * 3 + [pl.BlockSpec((tm, 16), row)]
    if emit_kv:
        out_shape += [f32(W)] * 4
        out_specs += [pl.BlockSpec((tm, W), row)] * 4
    return pl.pallas_call(
        functools.partial(_inproj_kernel, rope=rope, emit_kv=emit_kv),
        grid=(n // tm,), in_specs=in_specs, out_specs=out_specs, out_shape=out_shape,
        compiler_params=_cparams("parallel"), name="inproj",
    )(*args)


def _rope_tables(T):
    quarter = D_A // 4
    inv = 1.0 / (ROPE_BASE ** (jnp.arange(quarter, dtype=F32) / quarter))
    t = jnp.arange(T)
    pos = jnp.stack([t // GRID_W, t % GRID_W], axis=-1).astype(F32)
    ang = pos[:, :, None] * inv
    cos, sin = jnp.cos(ang), jnp.sin(ang)
    cos64 = jnp.concatenate([cos[:, 0], cos[:, 0], cos[:, 1], cos[:, 1]], axis=-1)
    zero = jnp.zeros_like(sin[:, 0])
    sa64 = jnp.concatenate([-sin[:, 0], zero, -sin[:, 1], zero], axis=-1)
    sb64 = jnp.concatenate([zero, sin[:, 0], zero, sin[:, 1]], axis=-1)
    rep = W_HEADS // D_A
    return tuple(jnp.tile(a, (1, rep)) for a in (cos64, sa64, sb64))


def _diff_attn_kernel(*refs, has_cache, lam_init):
    if has_cache:
        q_ref, k_ref, v_ref, kc_ref, vc_ref, lq_ref, g_ref, o_ref = refs
    else:
        q_ref, k_ref, v_ref, lq_ref, g_ref, o_ref = refs
    q = q_ref[...]
    lane = _iota(q.shape, 1)
    zero = jnp.zeros_like(q)
    qs = (jnp.where(lane < D_A, q, zero), jnp.where(lane >= D_A, q, zero))
    lq = lq_ref[...]
    lam = (jnp.exp(jnp.sum(lq[0:1] * lq[1:2], axis=-1, keepdims=True))
           - jnp.exp(jnp.sum(lq[2:3] * lq[3:4], axis=-1, keepdims=True)) + lam_init)
    hd = q.shape[1]

    def with_ones(v):
        one = jnp.where(_iota(v.shape, 1) == 0, 1.0, 0.0).astype(BF16)
        return jnp.concatenate([v, one], axis=1)

    T = k_ref.shape[0]
    tk = _pick_tile(T, ATTN_KEY_CHUNK)
    chunks = [(k_ref[j * tk:(j + 1) * tk, :], with_ones(v_ref[j * tk:(j + 1) * tk, :])) for j in range(T // tk)]
    if has_cache:
        chunks.append((kc_ref[...], with_ones(vc_ref[...])))
    tq = q.shape[0]
    outs = []
    for comp in range(2):
        m = jnp.full((tq, 1), NEG, F32)
        acc = jnp.zeros((tq, 2 * hd), F32)
        for kj, vxj in chunks:
            s = _dot_nt(qs[comp], kj)
            m_new = jnp.maximum(m, jnp.max(s, axis=-1, keepdims=True))
            acc = acc * jnp.exp2(m - m_new) + _dot(jnp.exp2(s - m_new).astype(BF16), vxj)
            m = m_new
        outs.append(acc[:, :hd] * (1.0 / acc[:, hd:hd + 1]))
    o = outs[0] - lam * outs[1]
    o_ref[...] = (_rms(o, g_ref[...]) * (1.0 - lam_init)).astype(BF16)


def _diff_attn(q, k, v, cache_k, cache_v, lam_qk, g_sub, B, T, lam_init):
    n = q.shape[0]
    tq = _pick_tile(T, 512)
    nq = T // tq
    has_cache = cache_k is not None
    hd = 2 * D_A
    in_specs = [pl.BlockSpec((tq, hd), lambda b, h, i: (b * nq + i, h)),
                pl.BlockSpec((T, hd), lambda b, h, i: (b, h)),
                pl.BlockSpec((T, hd), lambda b, h, i: (b, h))]
    args = [q, k, v]
    if has_cache:
        Lc = cache_k.shape[0] // B
        in_specs += [pl.BlockSpec((Lc, hd), lambda b, h, i: (b, h))] * 2
        args += [cache_k, cache_v]
    in_specs += [pl.BlockSpec(lam_qk.shape, lambda b, h, i: (0, 0)), pl.BlockSpec((1, hd), lambda b, h, i: (0, 0))]
    args += [lam_qk, g_sub]
    return pl.pallas_call(
        functools.partial(_diff_attn_kernel, has_cache=has_cache, lam_init=lam_init),
        grid=(B, H_A, nq), in_specs=in_specs,
        out_specs=pl.BlockSpec((tq, hd), lambda b, h, i: (b * nq + i, h)),
        out_shape=jax.ShapeDtypeStruct((n, W_HEADS), BF16),
        compiler_params=_cparams("parallel", "parallel", "parallel"), name="diff_attn",
    )(*args)


def _softmax_pv(heads):
    def fold(xs, op):
        out = xs[0]
        for x in xs[1:]:
            out = op(out, x)
        return out

    ms = [fold([jnp.max(s, axis=-1, keepdims=True) for s, _ in parts], jnp.maximum) for parts in heads]
    ps = [[jnp.exp(s - m) for s, _ in parts] for parts, m in zip(heads, ms)]
    rs = [1.0 / fold([jnp.sum(p, axis=-1, keepdims=True) for p in pp], jnp.add) for pp in ps]
    pbs = [[(p * r).astype(BF16) for p in pp] for pp, r in zip(ps, rs)]
    return [fold([_dot(p, v) for p, (_, v) in zip(pp, parts)], jnp.add) for pp, parts in zip(pbs, heads)]


def _na_kernel(*refs, rows, wr, rps):
    q_ref, k_ref, v_ref, kc_ref, vc_ref = refs[:5]
    bias_refs = refs[5:5 + rps]
    o_ref = refs[5 + rps]
    nloc = wr * GRID_W
    scale = D_C ** -0.5
    cols = [slice(h * D_C, (h + 1) * D_C) for h in range(H_C)]
    chains = []
    for i in range(rps):
        r = pl.program_id(1) * rps + i
        rs = jnp.clip(r - wr // 2, 0, rows - wr)
        start = pl.multiple_of(rs * GRID_W, GRID_W)
        chains += [(i, start, h) for h in range(H_C)]
    qrow = lambda i: slice(i * GRID_W, (i + 1) * GRID_W)
    qs = [q_ref[qrow(i), cols[h]] for i, _, h in chains]
    s_loc = [_dot_nt(q, k_ref[pl.ds(start, nloc), cols[h]]) for q, (_, start, h) in zip(qs, chains)]
    s_ctx = [_dot_nt(q, kc_ref[:, cols[h]]) for q, (_, _, h) in zip(qs, chains)]
    heads = [[(sl * scale + bias_refs[i][0, h], v_ref[pl.ds(start, nloc), cols[h]]),
              (sc * scale, vc_ref[:, cols[h]])] for sl, sc, (i, start, h) in zip(s_loc, s_ctx, chains)]
    for (i, _, h), o in zip(chains, _softmax_pv(heads)):
        o_ref[qrow(i), cols[h]] = o.astype(BF16)


def _na_bias_table(rpb, rows, wr):
    col = jnp.arange(GRID_W)
    cs = jnp.clip(col - WIN_C // 2, 0, GRID_W - WIN_C)
    col_mask = (col[None, :] >= cs[:, None]) & (col[None, :] < cs[:, None] + WIN_C)
    dc_idx = jnp.clip(col[None, :] - col[:, None] + WIN_C - 1, 0, 2 * WIN_C - 2)
    onehot = (dc_idx[:, :, None] == jnp.arange(2 * WIN_C - 1)).astype(F32)
    base = jnp.einsum('hdj,qkj->hdqk', rpb.astype(F32), onehot, precision=lax.Precision.HIGHEST)
    base = jnp.where(col_mask[None, None], base, NEG)
    tabs = []
    for off in range(wr):
        lo = WIN_R - 1 - off
        tabs.append(base[:, lo:lo + wr].transpose(0, 2, 1, 3).reshape(H_C, GRID_W, wr * GRID_W))
    return jnp.stack(tabs)


def _na_attn(q, k, v, cache_k, cache_v, rpb, B, T):
    n = q.shape[0]
    rows = T // GRID_W
    wr = min(WIN_R, rows)
    Lc = cache_k.shape[0] // B
    bias = _na_bias_table(rpb, rows, wr)
    W = W_HEADS

    rps = _pick_tile(rows, NA_ROWS_PER_STEP)
    nsteps = rows // rps

    def bias_spec(i):
        def idx(b, s):
            r = s * rps + i
            return (r - jnp.clip(r - wr // 2, 0, rows - wr), 0, 0, 0)
        return pl.BlockSpec((1, H_C, GRID_W, wr * GRID_W), idx)

    return pl.pallas_call(
        functools.partial(_na_kernel, rows=rows, wr=wr, rps=rps),
        grid=(B, nsteps),
        in_specs=[pl.BlockSpec((rps * GRID_W, W), lambda b, s: (b * nsteps + s, 0)),
                  pl.BlockSpec((T, W), lambda b, s: (b, 0)),
                  pl.BlockSpec((T, W), lambda b, s: (b, 0)),
                  pl.BlockSpec((Lc, W), lambda b, s: (b, 0)),
                  pl.BlockSpec((Lc, W), lambda b, s: (b, 0))] + [bias_spec(i) for i in range(rps)],
        out_specs=pl.BlockSpec((rps * GRID_W, W), lambda b, s: (b * nsteps + s, 0)),
        out_shape=jax.ShapeDtypeStruct((n, W), BF16),
        compiler_params=_cparams("parallel", "arbitrary"), name="na_attn",
    )(q, k, v, cache_k, cache_v, *([bias] * rps))


def _dense_attn_kernel(q_ref, k_ref, v_ref, o_ref):
    scale = D_C ** -0.5
    cols = [slice(h * D_C, (h + 1) * D_C) for h in range(H_C)]
    ss = [_dot_nt(q_ref[:, cs], k_ref[:, cs]) for cs in cols]
    heads = [[(s * scale, v_ref[:, cs])] for s, cs in zip(ss, cols)]
    for cs, o in zip(cols, _softmax_pv(heads)):
        o_ref[:, cs] = o.astype(BF16)


def _dense_attn(q, k, v, B, T):
    n = q.shape[0]
    tq = _pick_tile(T, 256)
    nq = T // tq
    W = W_HEADS
    return pl.pallas_call(
        _dense_attn_kernel, grid=(B, nq),
        in_specs=[pl.BlockSpec((tq, W), lambda b, i: (b * nq + i, 0)),
                  pl.BlockSpec((T, W), lambda b, i: (b, 0)),
                  pl.BlockSpec((T, W), lambda b, i: (b, 0))],
        out_specs=pl.BlockSpec((tq, W), lambda b, i: (b * nq + i, 0)),
        out_shape=jax.ShapeDtypeStruct((n, W), BF16),
        compiler_params=_cparams("parallel", "parallel"), name="dense_attn",
    )(q, k, v)


HALO = 8
DN_ROWS = H_B * CHUNK
INV_BASE = 8
DN_CHUNKS_PER_STEP = 2


def _bmm(a, b):
    return lax.dot_general(a, b, (((2,), (1,)), ((0,), (0,))), preferred_element_type=F32)


def _bmm_nt(a, b):
    return lax.dot_general(a, b, (((2,), (2,)), ((0,), (0,))), preferred_element_type=F32)


def _bmmb(a, b):
    return _bmm(a.astype(BF16), b.astype(BF16))


def _dn_prep_kernel(x_ref, xp_ref, xn_ref, ab_ref, abT_ref, cw_ref, alr_ref, dtr_ref, alc_ref, dtc_ref,
                    u_ref, wq_ref, kg_ref, a_ref, eg_ref, xs_ref):
    c = pl.program_id(1)
    nsteps = pl.num_programs(1)
    cps = abT_ref.shape[0]
    rows = cps * CHUNK
    pad = CONV_K // 2
    xs_ref[HALO:HALO + rows, :] = x_ref[...]
    xs_ref[0:HALO, :] = jnp.where(c > 0, xp_ref[...], 0.0)
    xs_ref[HALO + rows:, :] = jnp.where(c < nsteps - 1, xn_ref[...], 0.0)
    y = None
    for j in range(CONV_K):
        t = cw_ref[j:j + 1, :] * xs_ref[HALO - pad + j:HALO - pad + j + rows, :]
        y = t if y is None else y + t
    y = _silu(y)
    parts = []
    for j in range(cps):
        parts += _dn_chunk_terms(y[j * CHUNK:(j + 1) * CHUNK], ab_ref[j * CHUNK:(j + 1) * CHUNK, :], abT_ref[j],
                                 alr_ref, dtr_ref, alc_ref, dtc_ref, eg_ref, j)
    cat = lambda key: jnp.concatenate([p[key] for p in parts], axis=0)
    L = cat('L')
    ri = _iota((CHUNK, CHUNK), 0)
    ci = _iota((CHUNK, CHUNK), 1)
    inb = (ri // INV_BASE) == (ci // INV_BASE)
    Dg = jnp.where(inb, L, 0.0)
    X = (ri == ci).astype(F32) - Dg
    P = _bmmb(Dg, Dg)
    X = X + _bmmb(X, P)
    X = X + _bmmb(X, _bmmb(P, P))
    bs = INV_BASE
    while bs < CHUNK:
        outer = (ri // (2 * bs)) == (ci // (2 * bs))
        C = jnp.where(outer & jnp.logical_not(inb), L, 0.0)
        Xb = X.astype(BF16)
        X = X - _bmm(_bmm(Xb, C.astype(BF16)).astype(BF16), Xb)
        inb = outer
        bs *= 2
    Xb = X.astype(BF16)
    u = _bmm(Xb, cat('vb').astype(BF16))
    w = _bmm(Xb, cat('kbg').astype(BF16))
    for i, p in enumerate(parts):
        d, j = p['d'], p['j']
        rows_i = slice(i * H_B, (i + 1) * H_B)
        u_ref[d, j] = u[rows_i]
        wq_ref[d, j] = jnp.concatenate([w[rows_i], p['qg']], axis=1).astype(BF16)
        kg_ref[d, j] = p['kg'].astype(BF16)
        a_ref[d, j] = p['a_in'].astype(BF16)


def _dn_chunk_terms(y, ab, abT, alr_ref, dtr_ref, alc_ref, dtc_ref, eg_ref, j):
    W = W_HEADS

    def l2n(v):
        return v * lax.rsqrt(jnp.sum(v * v, axis=-1, keepdims=True) + EPS)

    q = jnp.stack([l2n(y[:, h * D_B:(h + 1) * D_B]) * (D_B ** -0.5) for h in range(H_B)])
    k = jnp.stack([l2n(y[:, W + h * D_B:W + (h + 1) * D_B]) for h in range(H_B)])
    v = jnp.stack([y[:, 2 * W + h * D_B:2 * W + (h + 1) * D_B] for h in range(H_B)])
    kb16 = k.astype(BF16)
    kk = _bmm_nt(kb16, kb16)
    qk = _bmm_nt(q.astype(BF16), kb16)

    nh2 = 2 * H_B
    g_col = -jnp.exp(alr_ref[...]) * _softplus(ab[:, 0:nh2] + dtr_ref[...])
    beta_col = _sigmoid(ab[:, nh2:2 * nh2])
    g_row = -jnp.exp(alc_ref[...]) * _softplus(abT[0:nh2, :] + dtc_ref[...])
    ri = _iota((CHUNK, CHUNK), 0)
    ci = _iota((CHUNK, CHUNK), 1)
    lo = (ri >= ci).astype(F32)
    up = (ri <= ci).astype(F32)
    gc = jnp.where(_iota((CHUNK, nh2), 1) < H_B, _dot_f32(lo, g_col), _dot_f32(up, g_col))
    gr = jnp.where(_iota((nh2, CHUNK), 0) < H_B, _dot_f32(g_row, up), _dot_f32(g_row, lo))
    glast = jnp.sum(g_col, axis=0, keepdims=True)
    eg_ref[j] = jnp.exp(jnp.sum(g_row, axis=1, keepdims=True)) * jnp.ones((nh2, LANES), F32)

    terms = []
    for d in range(2):
        cols = [d * H_B + h for h in range(H_B)]
        gc_h = jnp.stack([gc[:, i:i + 1] for i in cols])
        beta_h = jnp.stack([beta_col[:, i:i + 1] for i in cols])
        gl_h = jnp.stack([glast[:, i:i + 1] for i in cols])
        gr_h = jnp.stack([gr[i:i + 1, :] for i in cols])
        tri = (ri >= ci) if d == 0 else (ri <= ci)
        strict = (ri > ci) if d == 0 else (ri < ci)
        decay = jnp.where(tri, jnp.exp(jnp.where(tri, gc_h - gr_h, 0.0)), 0.0)
        egc = jnp.exp(gc_h)
        terms.append(dict(d=d, j=j,
                          L=jnp.where(strict, beta_h * kk * decay, 0.0),
                          a_in=jnp.where(tri, qk * decay, 0.0),
                          vb=v * beta_h, kbg=k * (beta_h * egc), qg=q * egc,
                          kg=k * jnp.exp(gl_h - gc_h)))
    return terms


def _dn_prep(xb, ab, conv_w, a_log, dt_bias, B, T):
    n = xb.shape[0]
    nc = T // CHUNK
    cps = _pick_tile(nc, DN_CHUNKS_PER_STEP)
    ns = nc // cps
    rows = cps * CHUNK
    nblk = n // HALO
    bps = rows // HALO
    W3 = xb.shape[1]
    nh2 = 2 * H_B
    abT3 = ab.reshape(n // CHUNK, CHUNK, 16).transpose(0, 2, 1)
    al = a_log.reshape(1, nh2)
    dt = dt_bias.reshape(1, nh2)
    st = lambda b, c: b * ns + c
    idx5 = lambda b, c: (0, st(b, c), 0, 0, 0)
    const = lambda b, c: (0, 0)
    return pl.pallas_call(
        _dn_prep_kernel, grid=(B, ns),
        in_specs=[pl.BlockSpec((rows, W3), lambda b, c: (st(b, c), 0)),
                  pl.BlockSpec((HALO, W3), lambda b, c: (jnp.maximum(st(b, c) * bps - 1, 0), 0)),
                  pl.BlockSpec((HALO, W3), lambda b, c: (jnp.minimum((st(b, c) + 1) * bps, nblk - 1), 0)),
                  pl.BlockSpec((rows, 16), lambda b, c: (st(b, c), 0)),
                  pl.BlockSpec((cps, 16, CHUNK), lambda b, c: (st(b, c), 0, 0)),
                  pl.BlockSpec(conv_w.shape, const),
                  pl.BlockSpec((1, nh2), const), pl.BlockSpec((1, nh2), const),
                  pl.BlockSpec((nh2, 1), const), pl.BlockSpec((nh2, 1), const)],
        out_specs=[pl.BlockSpec((2, cps, H_B, CHUNK, D_B), idx5),
                   pl.BlockSpec((2, cps, H_B, 2 * CHUNK, D_B), idx5),
                   pl.BlockSpec((2, cps, H_B, CHUNK, D_B), idx5),
                   pl.BlockSpec((2, cps, H_B, CHUNK, CHUNK), idx5),
                   pl.BlockSpec((cps, nh2, LANES), lambda b, c: (st(b, c), 0, 0))],
        out_shape=[jax.ShapeDtypeStruct((2, B * nc, H_B, CHUNK, D_B), F32),
                   jax.ShapeDtypeStruct((2, B * nc, H_B, 2 * CHUNK, D_B), BF16),
                   jax.ShapeDtypeStruct((2, B * nc, H_B, CHUNK, D_B), BF16),
                   jax.ShapeDtypeStruct((2, B * nc, H_B, CHUNK, CHUNK), BF16),
                   jax.ShapeDtypeStruct((B * nc, nh2, LANES), F32)],
        scratch_shapes=[pltpu.VMEM((rows + 2 * HALO, W3), F32)],
        compiler_params=_cparams("parallel", "parallel"), name="dn_prep",
    )(xb, xb, xb, ab, abT3, conv_w, al, dt, al.reshape(nh2, 1), dt.reshape(nh2, 1))


def _dn_scan_kernel(*refs):
    s0_ref = refs[0]
    per_dir = (refs[1:6], refs[6:11])
    of_ref, ob_ref, sfin_ref, s_ref = refs[11:15]
    c = pl.program_id(1)
    nc = pl.num_programs(1)

    @pl.when(c == 0)
    def _():
        s_ref[...] = s0_ref[0]

    cps = per_dir[0][4].shape[0]
    chains = [(d, h) for d in range(2) for h in range(H_B)]
    states = [s_ref[d, h] for d, h in chains]
    for sub in range(cps):
        js = [sub if d == 0 else cps - 1 - sub for d, _ in chains]
        rs = [_dot(per_dir[d][1][0, j, h], s.astype(BF16)) for (d, h), j, s in zip(chains, js, states)]
        vns = [(per_dir[d][0][0, j, h] - r[0:CHUNK]).astype(BF16) for (d, h), j, r in zip(chains, js, rs)]
        avs = [_dot(per_dir[d][3][0, j, h], vn) for (d, h), j, vn in zip(chains, js, vns)]
        dss = [_dot_tn(per_dir[d][2][0, j, h], vn) for (d, h), j, vn in zip(chains, js, vns)]
        new_states = []
        for i, ((d, h), j) in enumerate(zip(chains, js)):
            o_ref = of_ref if d == 0 else ob_ref
            o_ref[j * CHUNK:(j + 1) * CHUNK, h * D_B:(h + 1) * D_B] = rs[i][CHUNK:] + avs[i]
            eg = per_dir[d][4][j, d * H_B + h:d * H_B + h + 1, :]
            new_states.append(states[i] * eg + dss[i])
        states = new_states
    for (d, h), s in zip(chains, states):
        s_ref[d, h] = s

    @pl.when(c == nc - 1)
    def _():
        sfin_ref[0] = s_ref[...]


def _dn_scan(prep, s0, B, T):
    u, wq, kg, a, eg = prep
    nc = T // CHUNK
    R = DN_ROWS
    nh2 = 2 * H_B

    cps = _pick_tile(nc, DN_CHUNKS_PER_STEP)
    ns = nc // cps

    def dir_specs(d):
        cb = (lambda b, c: b * ns + c) if d == 0 else (lambda b, c: b * ns + ns - 1 - c)
        idx5 = lambda b, c: (d, cb(b, c), 0, 0, 0)
        return [pl.BlockSpec((1, cps, H_B, CHUNK, D_B), idx5),
                pl.BlockSpec((1, cps, H_B, 2 * CHUNK, D_B), idx5),
                pl.BlockSpec((1, cps, H_B, CHUNK, D_B), idx5),
                pl.BlockSpec((1, cps, H_B, CHUNK, CHUNK), idx5),
                pl.BlockSpec((cps, nh2, LANES), lambda b, c: (cb(b, c), 0, 0))]

    st = (1, 2, H_B, D_B, D_B)
    st_spec = pl.BlockSpec(st, lambda b, c: (b, 0, 0, 0, 0))
    return pl.pallas_call(
        _dn_scan_kernel, grid=(B, ns),
        in_specs=[st_spec] + dir_specs(0) + dir_specs(1),
        out_specs=[pl.BlockSpec((cps * CHUNK, W_HEADS), lambda b, c: (b * ns + c, 0)),
                   pl.BlockSpec((cps * CHUNK, W_HEADS), lambda b, c: (b * ns + ns - 1 - c, 0)),
                   st_spec],
        out_shape=[jax.ShapeDtypeStruct((B * T, W_HEADS), F32), jax.ShapeDtypeStruct((B * T, W_HEADS), F32),
                   jax.ShapeDtypeStruct((B,) + st[1:], F32)],
        scratch_shapes=[pltpu.VMEM(st[1:], F32)],
        compiler_params=_cparams("parallel", "arbitrary"), name="dn_scan",
    )(s0, u, wq, kg, a, eg, u, wq, kg, a, eg)


def _merge_kernel(x_ref, mod_ref, g1_ref, g2_ref, oa_ref, of_ref, ob_ref, zb_ref, gon_ref, oc_ref,
                  wg_ref, wa_ref, wb_ref, wc_ref, wo_ref, wrT_ref, x1_ref, h2_ref, lgT_ref):
    D = x_ref.shape[1]
    x = x_ref[...]
    sh1, sc1, gt1, sh2, sc2 = [mod_ref[0, j:j + 1, :] for j in range(5)]
    hb = (_rms(x, g1_ref[...]) * (1.0 + sc1) + sh1).astype(BF16)
    od = of_ref[...] + ob_ref[...]
    z = zb_ref[...]
    gon = gon_ref[...]
    obs = []
    for h in range(H_B):
        cs = slice(h * D_B, (h + 1) * D_B)
        obs.append(_rms(od[:, cs], gon) * _silu(z[:, cs]))
    ob = jnp.concatenate(obs, axis=1).astype(BF16)

    def gate(j):
        return _sigmoid(_dot(hb, wg_ref[:, j * D:(j + 1) * D]))

    merged = (gate(0) * _dot(oa_ref[...], wa_ref[...])
              + gate(1) * _dot(ob, wb_ref[...])
              + gate(2) * _dot(oc_ref[...], wc_ref[...]))
    x1 = x + gt1 * _dot(merged.astype(BF16), wo_ref[...])
    x1_ref[...] = x1
    h2 = _rms(x1, g2_ref[...]) * (1.0 + sc2) + sh2
    h2_ref[...] = h2
    E = lgT_ref.shape[0]
    h2h = h2.astype(BF16)
    h2l = (h2 - h2h.astype(F32)).astype(BF16)
    r1 = _dot_nt(wrT_ref[...], h2h)
    lgT_ref[...] = r1[:E] + (r1[E:] + _dot_nt(wrT_ref[0:E, :], h2l))


def _merge(x, mod5, g1, g2, oa, o_f, o_b, zb, g_onorm, oc, w_g, w_a, w_b, w_c, w_o, w_rT, T):
    n, D = x.shape
    tm = _pick_tile(T, 512)
    nt = T // tm
    W = W_HEADS
    E = w_rT.shape[0] // 2
    row = lambda i: (i, 0)
    const = lambda i: (0, 0)
    full = lambda a: pl.BlockSpec(a.shape, const)
    return pl.pallas_call(
        _merge_kernel, grid=(n // tm,),
        in_specs=[pl.BlockSpec((tm, D), row), pl.BlockSpec((1, 8, D), lambda i: (i // nt, 0, 0)),
                  pl.BlockSpec((1, D), const), pl.BlockSpec((1, D), const),
                  pl.BlockSpec((tm, W), row), pl.BlockSpec((tm, W), row), pl.BlockSpec((tm, W), row),
                  pl.BlockSpec((tm, W), row), pl.BlockSpec((1, D_B), const), pl.BlockSpec((tm, W), row),
                  full(w_g), full(w_a), full(w_b), full(w_c), full(w_o), full(w_rT)],
        out_specs=[pl.BlockSpec((tm, D), row), pl.BlockSpec((tm, D), row), pl.BlockSpec((E, tm), lambda i: (0, i))],
        out_shape=[jax.ShapeDtypeStruct((n, D), F32), jax.ShapeDtypeStruct((n, D), F32),
                   jax.ShapeDtypeStruct((E, n), F32)],
        compiler_params=_cparams("parallel"), name="merge",
    )(x, mod5, g1, g2, oa, o_f, o_b, zb, g_onorm, oc, w_g, w_a, w_b, w_c, w_o, w_rT)


def _route_kernel(lg_ref, idx_ref, w_ref, rc_ref, *, cap):
    E, R, _ = lg_ref.shape
    lg = lg_ref[...]
    m = jnp.max(lg, axis=0, keepdims=True)
    ex = jnp.exp(lg - m)
    aff = ex / jnp.sum(ex, axis=0, keepdims=True)
    bits = pltpu.bitcast(aff, I32)

    def count(mask):
        t = jnp.sum(jnp.where(mask, 1.0, 0.0), axis=2, keepdims=True)
        return jnp.sum(t, axis=1, keepdims=True)

    def bisect(_, carry):
        lo, hi = carry
        mid = lo + lax.shift_right_logical(hi - lo, 1)
        ge = count(bits >= mid) >= cap
        return jnp.where(ge, mid, lo), jnp.where(ge, hi, mid)

    lo0 = jnp.zeros((E, 1, 1), I32)
    hi0 = jnp.full((E, 1, 1), 0x7F800000, I32)
    thr, _ = lax.fori_loop(0, 31, bisect, (lo0, hi0))
    gt = bits > thr
    eq = bits == thr
    need = cap - count(gt)

    lane_r = _iota((LANES, LANES), 0)
    lane_c = _iota((LANES, LANES), 1)
    u_lane = (lane_r <= lane_c).astype(BF16)
    row_r = _iota((R, R), 0)
    row_c = _iota((R, R), 1)
    sl_row = (row_c < row_r).astype(BF16)
    u_row = (row_r <= row_c).astype(BF16)
    ones8 = jnp.ones((8, LANES), BF16)
    eye_l = (lane_r == lane_c).astype(BF16)

    def prefix(masks):
        ws = [_dot(mk.astype(BF16), u_lane) for mk in masks]
        tots = [(jnp.zeros((R, LANES), F32) + w[:, LANES - 1:LANES]).astype(BF16) for w in ws]
        return [w + _dot(sl_row, tot) for w, tot in zip(ws, tots)], ws

    gtf = jnp.where(gt, 1.0, 0.0)
    eqf = jnp.where(eq, 1.0, 0.0)
    experts = range(E)
    pes = prefix([eqf[e] for e in experts])[0]
    sels = [jnp.maximum(gtf[e], jnp.where(pes[e] <= need[e], eqf[e], 0.0)) for e in experts]
    ws = prefix(sels)[1]
    rts = [_dot_nt(ones8, s.astype(BF16)) for s in sels]
    rcs = [_dot(rt.astype(BF16), u_row) for rt in rts]
    for e in experts:
        w_ref[e] = ws[e]
        rc_ref[e] = rcs[e]

    nblk = cap // LANES
    G = _pick_tile(nblk, ROUTE_BLOCKS_PER_STEP)
    ngrp = nblk // G
    s_base = _iota((G * LANES, 1), 0).astype(F32)

    def slot_group(i, _):
        e = i // ngrp
        jg = i % ngrp
        w_e = w_ref[e]
        rc_row = rc_ref[e][0:1, :]
        s_col = s_base + jnp.asarray(jg * (G * LANES), F32)
        below = rc_row <= s_col
        rho = jnp.sum(jnp.where(below, 1.0, 0.0), axis=1, keepdims=True)
        before = jnp.max(jnp.where(below, rc_row, 0.0), axis=1, keepdims=True)
        onehot = (_iota((G * LANES, R), 1).astype(F32) == rho).astype(BF16)
        wg = _dot(onehot, w_e.astype(BF16))
        lam = jnp.sum(jnp.where(wg <= s_col - before, 1.0, 0.0), axis=1, keepdims=True)
        blocks = [slice(g * LANES, (g + 1) * LANES) for g in range(G)]
        rho_l = [_dot_nt(ones8, (eye_l * rho[b]).astype(BF16)) for b in blocks]
        lam_l = [_dot_nt(ones8, (eye_l * lam[b]).astype(BF16)) for b in blocks]
        for g in range(G):
            idx_ref[i * G + g] = (rho_l[g][0:1, :] * LANES + lam_l[g][0:1, :]).astype(I32)
        return 0

    lax.fori_loop(0, E * ngrp, slot_group, 0)


def _route(logits_t, cap):
    E, n = logits_t.shape
    R = n // LANES
    nblk = cap // LANES
    idx = pl.pallas_call(
        functools.partial(_route_kernel, cap=cap),
        out_shape=jax.ShapeDtypeStruct((E * nblk, 1, LANES), I32),
        scratch_shapes=[pltpu.VMEM((E, R, LANES), F32), pltpu.VMEM((E, 8, R), F32)],
        compiler_params=pltpu.CompilerParams(vmem_limit_bytes=VMEM_LIMIT_BYTES), name="route",
    )(logits_t.reshape(E, R, LANES))
    return idx.reshape(E, cap)


def _moe_kernel(idx_ref, idxn_ref, h_hbm, wr_ref, wg_ref, wu_ref, wd_ref, acc_in, acc_hbm,
                xbuf, mbuf, sem_x, sem_m, sem_s, *, nt):
    del acc_in
    e = pl.program_id(0)
    k = pl.program_id(1)
    total = pl.num_programs(0) * nt
    step = e * nt + k
    slot = step % 2
    tm = xbuf.shape[1]

    def row_copies(src_of, dst_of, sem, ids):
        def start(i, _):
            for j in range(ROW_COPY_UNROLL):
                r = i * ROW_COPY_UNROLL + j
                t = ids[0, 0, r]
                pltpu.make_async_copy(src_of(t, r), dst_of(t, r), sem).start(priority=j % 2)
            return 0
        lax.fori_loop(0, tm // ROW_COPY_UNROLL, start, 0)

    def wait_rows(sem, buf):
        pltpu.make_async_copy(buf, buf, sem).wait()

    def gather_x(ids, s):
        row_copies(lambda t, r: h_hbm.at[pl.ds(t, 1)], lambda t, r: xbuf.at[s, pl.ds(r, 1)], sem_x.at[s], ids)

    @pl.when(step == 0)
    def _():
        gather_x(idx_ref, 0)

    wait_rows(sem_x.at[slot], xbuf.at[slot])

    @pl.when(step + 1 < total)
    def _():
        gather_x(idxn_ref, 1 - slot)

    @pl.when(step > 0)
    def _():
        wait_rows(sem_s, mbuf)
    row_copies(lambda t, r: acc_hbm.at[pl.ds(t, 1)], lambda t, r: mbuf.at[pl.ds(r, 1)], sem_m, idx_ref)

    xe = xbuf[slot].astype(BF16)
    hid = _silu(_dot(xe, wg_ref[0])) * _dot(xe, wu_ref[0])
    lg = _dot(xe, wr_ref[...])
    lane = _iota(lg.shape, 1)
    lg = jnp.where(lane < N_EXPERTS, lg, NEG)
    ex = jnp.exp(lg - jnp.max(lg, axis=-1, keepdims=True))
    aff = ex / jnp.sum(ex, axis=-1, keepdims=True)
    gate = jnp.sum(jnp.where(lane == e, aff, 0.0), axis=-1, keepdims=True)
    ye = _dot(hid.astype(BF16), wd_ref[0]) * gate

    wait_rows(sem_m, mbuf)
    mbuf[...] = mbuf[...] + ye
    row_copies(lambda t, r: mbuf.at[pl.ds(r, 1)], lambda t, r: acc_hbm.at[pl.ds(t, 1)], sem_s, idx_ref)

    @pl.when(step == total - 1)
    def _():
        wait_rows(sem_s, mbuf)


def _moe(h2, idx, w_router_pad, w_gate, w_up, w_down, acc):
    n, D = h2.shape
    E, cap = idx.shape
    F = w_gate.shape[-1]
    tm = _pick_tile(cap, MOE_SLOT_TILE)
    nt = cap // tm
    idx2 = idx.reshape(E * nt, 1, tm)
    last = E * nt - 1
    smem_blk = lambda f: pl.BlockSpec((1, 1, tm), f, memory_space=pltpu.SMEM)
    return pl.pallas_call(
        functools.partial(_moe_kernel, nt=nt), grid=(E, nt),
        in_specs=[smem_blk(lambda e, k: (e * nt + k, 0, 0)),
                  smem_blk(lambda e, k: (jnp.minimum(e * nt + k + 1, last), 0, 0)),
                  pl.BlockSpec(memory_space=pl.ANY),
                  pl.BlockSpec(w_router_pad.shape, lambda e, k: (0, 0)),
                  pl.BlockSpec((1, D, F), lambda e, k: (e, 0, 0)),
                  pl.BlockSpec((1, D, F), lambda e, k: (e, 0, 0)),
                  pl.BlockSpec((1, F, D), lambda e, k: (e, 0, 0)),
                  pl.BlockSpec(memory_space=pl.ANY)],
        out_specs=pl.BlockSpec(memory_space=pl.ANY),
        out_shape=jax.ShapeDtypeStruct((n, D), F32),
        input_output_aliases={7: 0},
        scratch_shapes=[pltpu.VMEM((2, tm, D), F32), pltpu.VMEM((tm, D), F32),
                        pltpu.SemaphoreType.DMA((2,)), pltpu.SemaphoreType.DMA(()), pltpu.SemaphoreType.DMA(())],
        compiler_params=_cparams("arbitrary", "arbitrary"), name="moe",
    )(idx2, idx2, h2, w_router_pad, w_gate, w_up, w_down, acc)


def _resid_kernel(x_ref, m_ref, gt_ref, g_ref, o_ref, *, final):
    x2 = x_ref[...] + gt_ref[0] * m_ref[...]
    o_ref[...] = _rms(x2, g_ref[...]) if final else x2


def _resid(x1, m, gt2, g_final, T, final):
    n, D = x1.shape
    tm = _pick_tile(T, 512)
    nt = T // tm
    row = lambda i: (i, 0)
    return pl.pallas_call(
        functools.partial(_resid_kernel, final=final), grid=(n // tm,),
        in_specs=[pl.BlockSpec((tm, D), row), pl.BlockSpec((tm, D), row),
                  pl.BlockSpec((1, 1, D), lambda i: (i // nt, 0, 0)), pl.BlockSpec((1, D), lambda i: (0, 0))],
        out_specs=pl.BlockSpec((tm, D), row), out_shape=jax.ShapeDtypeStruct((n, D), F32),
        compiler_params=_cparams("parallel"), name="resid",
    )(x1, m, gt2, g_final)


def _lambda_init(layer):
    return 0.8 - 0.6 * math.exp(-0.3 * layer)


def _hi_lo_rows(w):
    hi = w.astype(BF16)
    return jnp.concatenate([hi, (w - hi.astype(F32)).astype(BF16)], axis=0)


def _prep_weights(p, l):
    W = W_HEADS
    D = p['w_in'].shape[1]
    w_in = p['w_in'][l]
    nab = 4 * H_B
    o_ab = 3 * W + 4 * W
    o_c = o_ab + nab
    o_g = o_c + 3 * W
    w_main = jnp.concatenate([w_in[:, :o_ab], w_in[:, o_c:o_g]], axis=1).astype(BF16)
    w_ab = w_in[:, o_ab:o_c].astype(BF16)
    E = p['w_router'].shape[-1]
    w_r = p['w_router'][l]
    return dict(
        w_main=w_main, w_ab=w_ab,
        w_g=w_in[:, o_g:].astype(BF16),
        w_a=p['w_br_a'][l].astype(BF16), w_b=p['w_br_b'][l].astype(BF16), w_c=p['w_br_c'][l].astype(BF16),
        w_o=p['w_out'][l].astype(BF16),
        w_rT=_hi_lo_rows(w_r.T),
        w_r_pad=jnp.pad(w_r, ((0, 0), (0, LANES - E))).astype(BF16),
        w_gate=p['w_e_gate'][l].astype(BF16), w_up=p['w_e_up'][l].astype(BF16), w_down=p['w_e_down'][l].astype(BF16),
        g1=p['g_norm1'][l][None], g2=p['g_norm2'][l][None],
        conv_w=p['conv_w'][l], a_log=p['a_log'][l], dt_bias=p['dt_bias'][l],
        g_onorm=p['g_onorm'][l][None], lam_qk=p['lam_qk'][l], g_subln=p['g_subln'][l][None], rpb=p['rpb'][l],
    )


def _trunk_layer(x, mod, lw, l, B, T, g_final, final, ctx=None):
    n, D = x.shape
    lam_init = _lambda_init(l)
    latent = ctx is not None
    sh1, sc1 = mod[:, 0:1], mod[:, 1:2]
    outs = _inproj(x, sc1, sh1, lw['g1'], lw['w_main'], lw['w_ab'], T,
                   _rope_tables(T) if latent else None, emit_kv=not latent)
    qa, ka, va, xb, zb, qc, kc, vc, ab = outs[:9]
    prep = _dn_prep(xb, ab, lw['conv_w'], lw['a_log'], lw['dt_bias'], B, T)
    if latent:
        ck_a, cv_a, ck_c, cv_c, c_state = ctx
        oa = _diff_attn(qa, ka, va, ck_a, cv_a, lw['lam_qk'], lw['g_subln'], B, T, lam_init)
        o_f, o_b, _ = _dn_scan(prep, c_state, B, T)
        oc = _na_attn(qc, kc, vc, ck_c, cv_c, lw['rpb'], B, T)
        new_ctx = None
    else:
        oa = _diff_attn(qa, ka, va, None, None, lw['lam_qk'], lw['g_subln'], B, T, lam_init)
        s0 = jnp.zeros((B, 2, H_B, D_B, D_B), F32)
        o_f, o_b, s_fin = _dn_scan(prep, s0, B, T)
        oc = _dense_attn(qc, kc, vc, B, T)
        new_ctx = tuple(a.reshape(B, T, 4, 128) for a in outs[9:13]) + (s_fin,)
    mod5 = jnp.pad(mod[:, :5], ((0, 0), (0, 3), (0, 0)))
    x1, h2, lg_t = _merge(x, mod5, lw['g1'], lw['g2'], oa, o_f, o_b, zb, lw['g_onorm'], oc,
                          lw['w_g'], lw['w_a'], lw['w_b'], lw['w_c'], lw['w_o'], lw['w_rT'], T)
    cap = EC_FACTOR * n // N_EXPERTS
    idx = _route(lg_t, cap)
    m = _moe(h2, idx, lw['w_r_pad'], lw['w_gate'], lw['w_up'], lw['w_down'], jnp.zeros((n, D), F32))
    x2 = _resid(x1, m, mod[:, 5:6], g_final, T, final)
    return x2, new_ctx


def kernel(x_prompt, x_sample, c, cache_diff_k, cache_diff_v, cache_na_k, cache_na_v, state_delta, c_ctx, w_ada, b_ada, g_norm1, g_norm2, w_in, conv_w, a_log, dt_bias, g_onorm, lam_qk, g_subln, rpb, w_br_a, w_br_b, w_br_c, w_out, w_router, w_e_gate, w_e_up, w_e_down, g_final):
    p = dict(w_in=w_in, conv_w=conv_w, a_log=a_log, dt_bias=dt_bias, g_onorm=g_onorm, lam_qk=lam_qk,
             g_subln=g_subln, rpb=rpb, w_br_a=w_br_a, w_br_b=w_br_b, w_br_c=w_br_c, w_out=w_out,
             w_router=w_router, w_e_gate=w_e_gate, w_e_up=w_e_up, w_e_down=w_e_down,
             g_norm1=g_norm1, g_norm2=g_norm2)
    depth = w_in.shape[0]
    Bp, Tp, D = x_prompt.shape
    Bs, Ts, _ = x_sample.shape
    gf = g_final[None]

    nrow = 1 + Bs
    cond = jnp.pad(jnp.concatenate([c_ctx[None], c], axis=0), ((0, -nrow % 8), (0, 0)))
    mod = _adaln(cond, w_ada, b_ada).reshape(depth, -1, 6, D)
    layers = [_prep_weights(p, l) for l in range(depth)]

    xp = x_prompt.reshape(Bp * Tp, D)
    new_ctx = []
    for l in range(depth):
        mod_ctx = jnp.broadcast_to(mod[l, 0:1], (Bp, 6, D))
        xp, nc = _trunk_layer(xp, mod_ctx, layers[l], l, Bp, Tp, gf, l == depth - 1)
        new_ctx.append(nc)

    xs = x_sample.reshape(Bs * Ts, D)
    for l in range(depth):
        Lc = cache_diff_k.shape[2]
        flat = lambda a: a[:, l].reshape(Bs * Lc, W_HEADS).astype(BF16)
        ctx = (flat(cache_diff_k), flat(cache_diff_v), flat(cache_na_k), flat(cache_na_v), state_delta[:, l])
        xs, _ = _trunk_layer(xs, mod[l, 1:1 + Bs], layers[l], l, Bs, Ts, gf, l == depth - 1, ctx)

    stack = lambda j: jnp.stack([nc[j] for nc in new_ctx], axis=1)
    return (xp.reshape(Bp, Tp, D), xs.reshape(Bs, Ts, D), stack(0), stack(1), stack(2), stack(3), stack(4))
```
